```python
import math
import jax
import jax.numpy as jnp
from jax import lax
import numpy as np

D_MODEL = 2048
BATCH = 8
SEQ = 2048
DEPTH = 1
DEC_BATCH = 128
DEC_SEQ = 4
PAST_LEN = 2048
PAGE_SIZE = 128

HEAD_DIM = 128
N_HEADS = D_MODEL // (2 * HEAD_DIM)
V_DIM = 2 * HEAD_DIM
QK_WIDTH = 2 * N_HEADS * HEAD_DIM
V_WIDTH = N_HEADS * V_DIM
D_CONV = D_MODEL // 2
CONV_W = 31
N_GROUPS = 4
EXPERTS_PER_GROUP = 8
N_EXPERTS = N_GROUPS * EXPERTS_PER_GROUP
TOP_K = 2
D_EXPERT = D_MODEL // 4
Q_BLOCK = 128
RMS_EPS = 1e-6
LN_EPS = 1e-5
ALIBI_MAX_EXP = 8.0
SPLIT_POINTS = (
    D_CONV,
    2 * D_CONV,
    2 * D_CONV + QK_WIDTH,
    2 * D_CONV + 2 * QK_WIDTH,
    2 * D_CONV + 2 * QK_WIDTH + V_WIDTH,
    2 * D_CONV + 2 * QK_WIDTH + V_WIDTH + D_MODEL,
)
D_IN = 2 * D_CONV + 2 * QK_WIDTH + V_WIDTH + 2 * D_MODEL

kernel_name = 'hybrid_conv_diffattn_hmoe_step'


def _rmsnorm(x, g):
    x32 = x.astype(jnp.float32)
    y = x32 * lax.rsqrt(jnp.mean(x32 * x32, axis=-1, keepdims=True) + RMS_EPS)
    return (y * g.astype(jnp.float32)).astype(x.dtype)


def _layernorm(x, g, b):
    x32 = x.astype(jnp.float32)
    mu = jnp.mean(x32, axis=-1, keepdims=True)
    xc = x32 - mu
    var = jnp.mean(xc * xc, axis=-1, keepdims=True)
    y = xc * lax.rsqrt(var + LN_EPS) * g.astype(jnp.float32) + b.astype(jnp.float32)
    return y.astype(x.dtype)


def _alibi_slopes():
    return jnp.exp2(-ALIBI_MAX_EXP * jnp.arange(1, N_HEADS + 1, dtype=jnp.float32) / N_HEADS)


def _lambda_init(layer):
    return 0.8 - 0.6 * math.exp(-0.3 * layer)


def _diff_attend(q, k, v, q_pos, k_pos, lam):
    s = jnp.einsum('bqhcd,bkhcd->bhcqk', q, k).astype(jnp.float32) * (HEAD_DIM ** -0.5)
    dist = (q_pos[:, None] - k_pos[None, :]).astype(jnp.float32)
    bias = jnp.where(dist[None] >= 0.0, -_alibi_slopes()[:, None, None] * dist[None], -jnp.inf)
    p = jax.nn.softmax(s + bias[None, :, None], axis=-1)
    a = p[:, :, 0] - lam * p[:, :, 1]
    return jnp.einsum('bhqk,bkhe->bqhe', a.astype(v.dtype), v)


def _attn_prompt(q, k, v, lam):
    b, s = q.shape[0], q.shape[1]
    nb = s // Q_BLOCK
    qb = q.reshape(b, nb, Q_BLOCK, N_HEADS, 2, HEAD_DIM).swapaxes(0, 1)
    k_pos = jnp.arange(s, dtype=jnp.int32)

    def block(args):
        q_i, i = args
        q_pos = i * Q_BLOCK + jnp.arange(Q_BLOCK, dtype=jnp.int32)
        return _diff_attend(q_i, k, v, q_pos, k_pos, lam)

    o = lax.map(block, (qb, jnp.arange(nb, dtype=jnp.int32)))
    return o.swapaxes(0, 1).reshape(b, s, N_HEADS, V_DIM)


def _attn_sample(q, k_new, v_new, lam, cache_k, cache_v, page_table, layer):
    n_pages = page_table.shape[1]
    past = n_pages * PAGE_SIZE
    ds = q.shape[1]
    k_pos = jnp.arange(past + ds, dtype=jnp.int32)
    q_pos = past + jnp.arange(ds, dtype=jnp.int32)

    def one(args):
        q_i, k_i, v_i, pt = args
        kp = cache_k[layer, pt].reshape(past, N_HEADS, 2, HEAD_DIM).astype(k_i.dtype)
        vp = cache_v[layer, pt].reshape(past, N_HEADS, V_DIM).astype(v_i.dtype)
        kk = jnp.concatenate([kp, k_i], axis=0)[None]
        vv = jnp.concatenate([vp, v_i], axis=0)[None]
        return _diff_attend(q_i[None], kk, vv, q_pos, k_pos, lam)[0]

    return lax.map(one, (q, k_new, v_new, page_table))


def _conv_branch(u_ext, dw_w, dw_b, ln_g, ln_b, w_pw):
    z = lax.conv_general_dilated(
        u_ext, dw_w[:, None, :].astype(u_ext.dtype), window_strides=(1,), padding='VALID',
        dimension_numbers=('NWC', 'WIO', 'NWC'), feature_group_count=D_CONV) + dw_b
    z = _layernorm(z, ln_g, ln_b)
    return jax.nn.silu(z) @ w_pw


def _moe_tokens(h, rg_w, rg_b, re_w, re_b, wg, wu, wd):
    lg = (h @ rg_w + rg_b).astype(jnp.float32)
    pg = jax.nn.softmax(lg, axis=-1)
    g_sel = jnp.argmax(lg, axis=-1)
    p_grp = jnp.take_along_axis(pg, g_sel[:, None], axis=-1)
    le = (h @ re_w + re_b).astype(jnp.float32).reshape(-1, N_GROUPS, EXPERTS_PER_GROUP)
    le_sel = jnp.take_along_axis(le, g_sel[:, None, None], axis=1)[:, 0]
    pe = jax.nn.softmax(le_sel, axis=-1)
    w2, i2 = lax.top_k(pe, TOP_K)
    w2 = w2 / jnp.sum(w2, axis=-1, keepdims=True) * p_grp
    eid = g_sel[:, None] * EXPERTS_PER_GROUP + i2
    gates = jnp.sum(jax.nn.one_hot(eid, N_EXPERTS, dtype=jnp.float32) * w2[..., None], axis=1)
    hg = jnp.einsum('td,edf->tef', h, wg)
    hu = jnp.einsum('td,edf->tef', h, wu)
    act = jax.nn.silu(hg) * hu * gates[..., None].astype(h.dtype)
    return jnp.einsum('tef,efd->td', act, wd)


def _moe_apply(h, flat, lp):
    shape = h.shape
    rows = h.reshape(1, -1, shape[-1]) if flat else h
    out = lax.map(lambda r: _moe_tokens(r, lp['router_group_w'], lp['router_group_b'],
                                        lp['router_expert_w'], lp['router_expert_b'],
                                        lp['expert_w_gate'], lp['expert_w_up'], lp['expert_w_down']), rows)
    return out.reshape(shape)


def _layer(x, conv_prev, attend_fn, lam_init, flat_tokens, lp):
    bsz, t = x.shape[0], x.shape[1]
    h = _rmsnorm(x, lp['norm_mix_g'])
    z = h @ lp['w_in']
    glu_a, glu_b, q, k, v, g_conv, g_attn = jnp.split(z, SPLIT_POINTS, axis=-1)
    u = glu_a * jax.nn.sigmoid(glu_b)
    u_ext = jnp.concatenate([conv_prev.astype(u.dtype), u], axis=1)
    conv_out = _conv_branch(u_ext, lp['conv_dw_w'], lp['conv_dw_b'], lp['conv_ln_g'],
                            lp['conv_ln_b'], lp['conv_w_out'])
    q = q.reshape(bsz, t, N_HEADS, 2, HEAD_DIM)
    k = k.reshape(bsz, t, N_HEADS, 2, HEAD_DIM)
    v = v.reshape(bsz, t, N_HEADS, V_DIM)
    f32 = jnp.float32
    lam = (jnp.exp(jnp.sum(lp['lambda_q1'].astype(f32) * lp['lambda_k1'].astype(f32)))
           - jnp.exp(jnp.sum(lp['lambda_q2'].astype(f32) * lp['lambda_k2'].astype(f32)))
           + lam_init)
    o = attend_fn(q, k, v, lam)
    o = _rmsnorm(o, lp['head_norm_g']) * (1.0 - lam_init)
    attn_out = o.reshape(bsz, t, V_WIDTH) @ lp['attn_w_out']
    merged = jax.nn.sigmoid(g_conv) * conv_out + jax.nn.sigmoid(g_attn) * attn_out
    x = x + merged @ lp['w_out']
    x = x + _moe_apply(_rmsnorm(x, lp['norm_ffn_g']), flat_tokens, lp)
    return x, k.reshape(bsz, t, N_HEADS, 2 * HEAD_DIM), v, u_ext[:, -(CONV_W - 1):]


def setup_inputs(seed: int = 0) -> dict:
    key = jax.random.key(seed)
    ks = iter(jax.random.split(key, 40))
    f32 = jnp.float32

    def nrm(shape, scale):
        return jax.random.normal(next(ks), shape, f32) * scale

    n_pages = PAST_LEN // PAGE_SIZE
    n_used = DEC_BATCH * n_pages
    n_pool = n_used + max(1, n_used // 4)
    page_table = jax.random.permutation(next(ks), n_pool)[:n_used].reshape(DEC_BATCH, n_pages).astype(jnp.int32)
    return {
        'x_prompt': nrm((BATCH, SEQ, D_MODEL), 1.0),
        'x_sample': nrm((DEC_BATCH, DEC_SEQ, D_MODEL), 1.0),
        'cache_k': nrm((DEPTH, n_pool, PAGE_SIZE, N_HEADS, 2 * HEAD_DIM), 1.0),
        'cache_v': nrm((DEPTH, n_pool, PAGE_SIZE, N_HEADS, V_DIM), 1.0),
        'state_conv': nrm((DEPTH, DEC_BATCH, CONV_W - 1, D_CONV), 0.5),
        'page_table': page_table,
        'norm_mix_g': 1.0 + nrm((DEPTH, D_MODEL), 0.02),
        'w_in': nrm((DEPTH, D_MODEL, D_IN), D_MODEL ** -0.5),
        'conv_dw_w': nrm((DEPTH, CONV_W, D_CONV), CONV_W ** -0.5),
        'conv_dw_b': nrm((DEPTH, D_CONV), 0.02),
        'conv_ln_g': 1.0 + nrm((DEPTH, D_CONV), 0.02),
        'conv_ln_b': nrm((DEPTH, D_CONV), 0.02),
        'conv_w_out': nrm((DEPTH, D_CONV, D_MODEL), D_CONV ** -0.5),
        'lambda_q1': nrm((DEPTH, HEAD_DIM), 0.1),
        'lambda_k1': nrm((DEPTH, HEAD_DIM), 0.1),
        'lambda_q2': nrm((DEPTH, HEAD_DIM), 0.1),
        'lambda_k2': nrm((DEPTH, HEAD_DIM), 0.1),
        'head_norm_g': 1.0 + nrm((DEPTH, V_DIM), 0.02),
        'attn_w_out': nrm((DEPTH, V_WIDTH, D_MODEL), V_WIDTH ** -0.5),
        'w_out': nrm((DEPTH, D_MODEL, D_MODEL), D_MODEL ** -0.5),
        'norm_ffn_g': 1.0 + nrm((DEPTH, D_MODEL), 0.02),
        'router_group_w': nrm((DEPTH, D_MODEL, N_GROUPS), D_MODEL ** -0.5),
        'router_group_b': nrm((DEPTH, N_GROUPS), 0.01),
        'router_expert_w': nrm((DEPTH, D_MODEL, N_EXPERTS), D_MODEL ** -0.5),
        'router_expert_b': nrm((DEPTH, N_EXPERTS), 0.01),
        'expert_w_gate': nrm((DEPTH, N_EXPERTS, D_MODEL, D_EXPERT), D_MODEL ** -0.5),
        'expert_w_up': nrm((DEPTH, N_EXPERTS, D_MODEL, D_EXPERT), D_MODEL ** -0.5),
        'expert_w_down': nrm((DEPTH, N_EXPERTS, D_EXPERT, D_MODEL), D_EXPERT ** -0.5),
        'final_norm_g': 1.0 + nrm((D_MODEL,), 0.02),
    }


def reference(x_prompt, x_sample, cache_k, cache_v, state_conv, page_table,
              norm_mix_g, w_in, conv_dw_w, conv_dw_b, conv_ln_g, conv_ln_b, conv_w_out,
              lambda_q1, lambda_k1, lambda_q2, lambda_k2, head_norm_g, attn_w_out, w_out,
              norm_ffn_g, router_group_w, router_group_b, router_expert_w, router_expert_b,
              expert_w_gate, expert_w_up, expert_w_down, final_norm_g):
    xp, xs = x_prompt, x_sample
    kp_l, vp_l, cp_l, ks_l, vs_l, cs_l = [], [], [], [], [], []
    for l in range(DEPTH):
        lp = dict(norm_mix_g=norm_mix_g[l], w_in=w_in[l], conv_dw_w=conv_dw_w[l],
                  conv_dw_b=conv_dw_b[l], conv_ln_g=conv_ln_g[l], conv_ln_b=conv_ln_b[l],
                  conv_w_out=conv_w_out[l], lambda_q1=lambda_q1[l], lambda_k1=lambda_k1[l],
                  lambda_q2=lambda_q2[l], lambda_k2=lambda_k2[l], head_norm_g=head_norm_g[l],
                  attn_w_out=attn_w_out[l], w_out=w_out[l], norm_ffn_g=norm_ffn_g[l],
                  router_group_w=router_group_w[l], router_group_b=router_group_b[l],
                  router_expert_w=router_expert_w[l], router_expert_b=router_expert_b[l],
                  expert_w_gate=expert_w_gate[l], expert_w_up=expert_w_up[l],
                  expert_w_down=expert_w_down[l])
        lam_init = _lambda_init(l)
        zero_ctx = jnp.zeros((xp.shape[0], CONV_W - 1, D_CONV), xp.dtype)
        xp, k_p, v_p, c_p = _layer(xp, zero_ctx, _attn_prompt, lam_init, False, lp)
        attend_s = lambda q, k, v, lam, l=l: _attn_sample(q, k, v, lam, cache_k, cache_v, page_table, l)
        xs, k_s, v_s, c_s = _layer(xs, state_conv[l], attend_s, lam_init, True, lp)
        kp_l.append(k_p); vp_l.append(v_p); cp_l.append(c_p)
        ks_l.append(k_s); vs_l.append(v_s); cs_l.append(c_s)
    y_prompt = _rmsnorm(xp, final_norm_g)
    y_sample = _rmsnorm(xs, final_norm_g)
    return (y_prompt, y_sample, jnp.stack(kp_l), jnp.stack(vp_l), jnp.stack(cp_l),
            jnp.stack(ks_l), jnp.stack(vs_l), jnp.stack(cs_l))
```

```python
import functools
import math

import jax
import jax.numpy as jnp
from jax import lax
from jax.experimental import pallas as pl
from jax.experimental.pallas import tpu as pltpu

F32 = jnp.float32
BF16 = jnp.bfloat16
I32 = jnp.int32

D_MODEL = 2048
HEAD_DIM = 128
N_HEADS = D_MODEL // (2 * HEAD_DIM)
V_DIM = 2 * HEAD_DIM
D_CONV = D_MODEL // 2
CONV_W = 31
N_GROUPS = 4
EXPERTS_PER_GROUP = 8
N_EXPERTS = N_GROUPS * EXPERTS_PER_GROUP
D_EXPERT = D_MODEL // 4
PAGE_SIZE = 128
RMS_EPS = 1e-6
LN_EPS = 1e-5
ALIBI_MAX_EXP = 8.0
LAYER = 0
LAM_INIT = 0.8 - 0.6 * math.exp(-0.3 * LAYER)

LANES = 128
NEG_BIG = -1e30
VMEM_LIMIT = 56 * 1024 * 1024


def _cparams(sem):
    return pltpu.CompilerParams(dimension_semantics=sem, vmem_limit_bytes=VMEM_LIMIT)


def _full(shape):
    return pl.BlockSpec(shape, lambda *_: (0,) * len(shape))


def _norm_matmul_kernel(x_ref, g_ref, *refs, n_w, n_out, epilogue):
    w_refs = refs[:n_w]
    o_refs = refs[n_w:n_w + n_out]
    h_ref = refs[n_w + n_out]

    @pl.when(pl.program_id(1) == 0)
    def _():
        x = x_ref[...]
        inv = lax.rsqrt(jnp.mean(x * x, axis=-1, keepdims=True) + RMS_EPS)
        h_ref[...] = (x * inv * g_ref[...]).astype(BF16)

    h = h_ref[...]
    zs = [jnp.dot(h, w[...], preferred_element_type=F32) for w in w_refs]
    for o_ref, val in zip(o_refs, epilogue(*zs)):
        o_ref[...] = val.astype(o_ref.dtype)


def _norm_matmul(x, g, ws, epilogue, out_dtypes, tm, tn, name):
    t, d = x.shape
    n = ws[0].shape[1]
    kern = functools.partial(_norm_matmul_kernel, n_w=len(ws), n_out=len(out_dtypes), epilogue=epilogue)
    return pl.pallas_call(
        kern,
        grid=(t // tm, n // tn),
        in_specs=[pl.BlockSpec((tm, d), lambda i, j: (i, 0)), _full((1, d))]
        + [pl.BlockSpec((d, tn), lambda i, j: (0, j)) for _ in ws],
        out_specs=[pl.BlockSpec((tm, tn), lambda i, j: (i, j)) for _ in out_dtypes],
        out_shape=[jax.ShapeDtypeStruct((t, n), dt) for dt in out_dtypes],
        scratch_shapes=[pltpu.VMEM((tm, d), BF16)],
        compiler_params=_cparams(("parallel", "arbitrary")),
        name=name,
    )(x, g, *ws)


def _glu_epilogue(a, b):
    return (a * jax.nn.sigmoid(b),)


def _qkv_epilogue(q, k, v):
    return (q * (HEAD_DIM ** -0.5), k, k, v, v)


def _gate_epilogue(gc, ga):
    return (jax.nn.sigmoid(gc), jax.nn.sigmoid(ga))


def _ln_swish_project(z, lng_ref, lnb_ref, wpw_ref):
    mu = jnp.mean(z, axis=-1, keepdims=True)
    zc = z - mu
    var = jnp.mean(zc * zc, axis=-1, keepdims=True)
    y = zc * lax.rsqrt(var + LN_EPS) * lng_ref[...] + lnb_ref[...]
    y = y * jax.nn.sigmoid(y)
    return jnp.dot(y.astype(BF16), wpw_ref[...], preferred_element_type=F32)


HALO = 32


def _conv_prompt_kernel(u_ref, halo_ref, dww_ref, dwb_ref, lng_ref, lnb_ref, wpw_ref, o_ref, ext_ref, z_ref, *, ts):
    first = pl.program_id(1) == 0
    ext_ref[0:HALO, :] = jnp.where(first, 0.0, halo_ref[...])
    ext_ref[HALO:, :] = u_ref[...]
    base = HALO - (CONV_W - 1)
    for cb in range(D_CONV // LANES):
        cs = slice(cb * LANES, (cb + 1) * LANES)
        acc = jnp.broadcast_to(dwb_ref[:, cs], (ts, LANES))
        for j in range(CONV_W):
            acc = acc + dww_ref[j:j + 1, cs] * ext_ref[base + j:base + j + ts, cs]
        z_ref[:, cs] = acc
    o_ref[...] = _ln_swish_project(z_ref[...], lng_ref, lnb_ref, wpw_ref)


def _conv_prompt(u, dww, dwb, lng, lnb, wpw, ts=256):
    b, s, c = u.shape
    d = wpw.shape[1]
    r = ts // HALO
    return pl.pallas_call(
        functools.partial(_conv_prompt_kernel, ts=ts),
        grid=(b, s // ts),
        in_specs=[
            pl.BlockSpec((None, ts, c), lambda bi, i: (bi, i, 0)),
            pl.BlockSpec((None, HALO, c), lambda bi, i: (bi, jnp.maximum(i * r - 1, 0), 0)),
            _full((CONV_W, c)), _full((1, c)), _full((1, c)), _full((1, c)), _full((c, d)),
        ],
        out_specs=pl.BlockSpec((None, ts, d), lambda bi, i: (bi, i, 0)),
        out_shape=jax.ShapeDtypeStruct((b, s, d), F32),
        scratch_shapes=[pltpu.VMEM((ts + HALO, c), F32), pltpu.VMEM((ts, c), F32)],
        compiler_params=_cparams(("parallel", "arbitrary")),
        name="conv_prompt",
    )(u, u, dww, dwb, lng, lnb, wpw)


def _conv_sample_kernel(ext_ref, dww_ref, dwb_ref, lng_ref, lnb_ref, wpw_ref, o_ref, *, n_new, bt):
    zs = []
    for t in range(n_new):
        acc = jnp.broadcast_to(dwb_ref[...], (bt, D_CONV))
        for j in range(CONV_W):
            acc = acc + dww_ref[j:j + 1, :] * ext_ref[t + j]
        zs.append(acc)
    out = _ln_swish_project(jnp.concatenate(zs, axis=0), lng_ref, lnb_ref, wpw_ref)
    for t in range(n_new):
        o_ref[t] = out[t * bt:(t + 1) * bt]


def _conv_sample(ext_t, dww, dwb, lng, lnb, wpw, bt=32):
    rows, b, c = ext_t.shape
    n_new = rows - (CONV_W - 1)
    d = wpw.shape[1]
    return pl.pallas_call(
        functools.partial(_conv_sample_kernel, n_new=n_new, bt=bt),
        grid=(b // bt,),
        in_specs=[
            pl.BlockSpec((rows, bt, c), lambda i: (0, i, 0)),
            _full((CONV_W, c)), _full((1, c)), _full((1, c)), _full((1, c)), _full((c, d)),
        ],
        out_specs=pl.BlockSpec((n_new, bt, d), lambda i: (0, i, 0)),
        out_shape=jax.ShapeDtypeStruct((n_new, b, d), F32),
        compiler_params=_cparams(("parallel",)),
        name="conv_sample",
    )(ext_t, dww, dwb, lng, lnb, wpw)


def _lambda_value(lq1_ref, lk1_ref, lq2_ref, lk2_ref):
    a = jnp.sum(lq1_ref[...] * lk1_ref[...], axis=-1, keepdims=True)
    b = jnp.sum(lq2_ref[...] * lk2_ref[...], axis=-1, keepdims=True)
    return jnp.exp(a) - jnp.exp(b) + LAM_INIT


def _head_norm(o, g_ref):
    inv = lax.rsqrt(jnp.mean(o * o, axis=-1, keepdims=True) + RMS_EPS)
    return o * inv * g_ref[...] * (1.0 - LAM_INIT)


def _alibi_slope(h, shape):
    return jnp.exp2(jnp.full(shape, -(ALIBI_MAX_EXP / N_HEADS), F32) * (h + 1).astype(F32))


def _nt_dot(a, b):
    return lax.dot_general(a, b, (((1,), (1,)), ((), ())), preferred_element_type=F32)


def _attn_prompt_kernel(q_ref, k_ref, v_ref, lq1_ref, lk1_ref, lq2_ref, lk2_ref, g_ref, o_ref,
                        m_ref, l_ref, acc_ref, *, tq):
    h = pl.program_id(1)
    i = pl.program_id(2)
    slope = _alibi_slope(h, (1, tq))
    q = q_ref[...]
    col = lax.broadcasted_iota(I32, (1, tq), 1)

    def block(kb, masked, first):
        ks = k_ref[pl.ds(kb * tq, tq), :]
        vs = v_ref[pl.ds(kb * tq, tq), :]
        bias = slope * (kb * tq + col).astype(F32)
        if masked:
            row = lax.broadcasted_iota(I32, (tq, tq), 0)
            visible = lax.broadcasted_iota(I32, (tq, tq), 1) <= row
        for c in range(2):
            cs = slice(c * HEAD_DIM, (c + 1) * HEAD_DIM)
            s = _nt_dot(q[:, cs], ks[:, cs]) + bias
            if masked:
                s = jnp.where(visible, s, -jnp.inf)
            m_cur = jnp.max(s, axis=-1, keepdims=True)
            if first:
                m_new = m_cur
                p = jnp.exp(s - m_new)
                l_new = jnp.sum(p, axis=-1, keepdims=True)
                acc_new = jnp.dot(p.astype(BF16), vs, preferred_element_type=F32)
            else:
                m_prev = m_ref[c][:, :1]
                m_new = jnp.maximum(m_prev, m_cur)
                alpha = jnp.exp(m_prev - m_new)
                p = jnp.exp(s - m_new)
                l_new = alpha * l_ref[c][:, :1] + jnp.sum(p, axis=-1, keepdims=True)
                acc_new = alpha * acc_ref[c] + jnp.dot(p.astype(BF16), vs, preferred_element_type=F32)
            m_ref[c] = jnp.broadcast_to(m_new, (tq, LANES))
            l_ref[c] = jnp.broadcast_to(l_new, (tq, LANES))
            acc_ref[c] = acc_new

    block(i, True, True)

    def body(kb, carry):
        block(kb, False, False)
        return carry

    lax.fori_loop(0, i, body, 0)

    lam = _lambda_value(lq1_ref, lk1_ref, lq2_ref, lk2_ref)
    o = acc_ref[0] / l_ref[0][:, :1] - lam * (acc_ref[1] / l_ref[1][:, :1])
    o_ref[...] = _head_norm(o, g_ref).astype(o_ref.dtype)


def _attn_prompt(q, k, v, lams, g, tq=256):
    b, s, _ = q.shape
    w = 2 * HEAD_DIM
    kv_spec = pl.BlockSpec((None, s, w), lambda bi, h, i: (bi, 0, h))
    return pl.pallas_call(
        functools.partial(_attn_prompt_kernel, tq=tq),
        grid=(b, N_HEADS, s // tq),
        in_specs=[pl.BlockSpec((None, tq, w), lambda bi, h, i: (bi, i, h)), kv_spec, kv_spec]
        + [_full((1, HEAD_DIM))] * 4 + [_full((1, V_DIM))],
        out_specs=pl.BlockSpec((None, tq, V_DIM), lambda bi, h, i: (bi, i, h)),
        out_shape=jax.ShapeDtypeStruct((b, s, N_HEADS * V_DIM), BF16),
        scratch_shapes=[pltpu.VMEM((2, tq, LANES), F32), pltpu.VMEM((2, tq, LANES), F32),
                        pltpu.VMEM((2, tq, V_DIM), F32)],
        compiler_params=_cparams(("parallel", "parallel", "arbitrary")),
        name="attn_prompt",
    )(q, k, v, *lams, g)


NEW_PAD = 128


def _attn_sample_kernel(pt_ref, q_ref, kn_ref, vn_ref, *refs, pp, n_new, n_steps):
    k_refs = refs[:pp]
    v_refs = refs[pp:2 * pp]
    lq1_ref, lk1_ref, lq2_ref, lk2_ref, g_ref, o_ref, kb_ref, vb_ref, m_ref, l_ref, acc_ref = refs[2 * pp:]
    step = pl.program_id(1)
    rows = 2 * n_new

    @pl.when(step == 0)
    def _():
        m_ref[...] = jnp.full(m_ref.shape, NEG_BIG, F32)
        l_ref[...] = jnp.zeros(l_ref.shape, F32)
        acc_ref[...] = jnp.zeros(acc_ref.shape, F32)

    def update(h, s, vh):
        m_prev = m_ref[h][:, :1]
        m_new = jnp.maximum(m_prev, jnp.max(s, axis=-1, keepdims=True))
        alpha = jnp.exp(m_prev - m_new)
        p = jnp.exp(s - m_new)
        l_new = alpha * l_ref[h][:, :1] + jnp.sum(p, axis=-1, keepdims=True)
        acc_ref[h] = alpha * acc_ref[h] + jnp.dot(p.astype(BF16), vh, preferred_element_type=F32)
        m_ref[h] = jnp.broadcast_to(m_new, (rows, LANES))
        l_ref[h] = jnp.broadcast_to(l_new, (rows, LANES))

    for j in range(pp):
        kb_ref[j * PAGE_SIZE:(j + 1) * PAGE_SIZE, :] = k_refs[j][...].astype(BF16)
        vb_ref[j * PAGE_SIZE:(j + 1) * PAGE_SIZE, :] = v_refs[j][...].astype(BF16)
    nk = pp * PAGE_SIZE
    kpos = (step * nk + lax.broadcasted_iota(I32, (1, nk), 1)).astype(F32)
    for h in range(N_HEADS):
        hs = slice(h * V_DIM, (h + 1) * V_DIM)
        slope = 2.0 ** (-(ALIBI_MAX_EXP / N_HEADS) * (h + 1))
        s = _nt_dot(q_ref[h], kb_ref[:, hs]) + slope * kpos
        update(h, s, vb_ref[:, hs])

    @pl.when(step == n_steps - 1)
    def _():
        past = n_steps * nk
        pad = jnp.zeros((NEW_PAD - kn_ref.shape[0], kn_ref.shape[1]), BF16)
        kn = jnp.concatenate([kn_ref[...], pad], axis=0)
        vn = jnp.concatenate([vn_ref[...], pad], axis=0)
        key = lax.broadcasted_iota(I32, (rows, NEW_PAD), 1)
        qi = lax.broadcasted_iota(I32, (rows, NEW_PAD), 0) % n_new
        visible = key <= qi
        lam = _lambda_value(lq1_ref, lk1_ref, lq2_ref, lk2_ref)
        for h in range(N_HEADS):
            hs = slice(h * V_DIM, (h + 1) * V_DIM)
            slope = 2.0 ** (-(ALIBI_MAX_EXP / N_HEADS) * (h + 1))
            s = _nt_dot(q_ref[h], kn[:, hs]) + slope * (past + key).astype(F32)
            update(h, jnp.where(visible, s, -jnp.inf), vn[:, hs])
            a = acc_ref[h] / l_ref[h][:, :1]
            o = a[:n_new] - lam * a[n_new:]
            o_ref[:, hs] = _head_norm(o, g_ref).astype(o_ref.dtype)


def _attn_sample(qbd, kn, vn, cache_k, cache_v, page_table, lams, g, pp=4):
    db, n_pages = page_table.shape
    n_new = qbd.shape[2] // 2
    n_steps = n_pages // pp
    width = N_HEADS * V_DIM

    def page_spec(j):
        return pl.BlockSpec((None, PAGE_SIZE, width),
                            lambda b, s, pt: (pt[b * n_pages + s * pp + j], 0, 0))

    const = lambda shape: pl.BlockSpec(shape, lambda b, s, pt: (0,) * len(shape))
    grid_spec = pltpu.PrefetchScalarGridSpec(
        num_scalar_prefetch=1,
        grid=(db, n_steps),
        in_specs=[pl.BlockSpec((None, N_HEADS, 2 * n_new, V_DIM), lambda b, s, pt: (b, 0, 0, 0)),
                  pl.BlockSpec((None, kn.shape[1], width), lambda b, s, pt: (b, 0, 0)),
                  pl.BlockSpec((None, kn.shape[1], width), lambda b, s, pt: (b, 0, 0))]
        + [page_spec(j) for j in range(pp)] + [page_spec(j) for j in range(pp)]
        + [const((1, HEAD_DIM))] * 4 + [const((1, V_DIM))],
        out_specs=pl.BlockSpec((None, n_new, width), lambda b, s, pt: (b, 0, 0)),
        scratch_shapes=[pltpu.VMEM((pp * PAGE_SIZE, width), BF16), pltpu.VMEM((pp * PAGE_SIZE, width), BF16),
                        pltpu.VMEM((N_HEADS, 2 * n_new, LANES), F32), pltpu.VMEM((N_HEADS, 2 * n_new, LANES), F32),
                        pltpu.VMEM((N_HEADS, 2 * n_new, V_DIM), F32)],
    )
    return pl.pallas_call(
        functools.partial(_attn_sample_kernel, pp=pp, n_new=n_new, n_steps=n_steps),
        grid_spec=grid_spec,
        out_shape=jax.ShapeDtypeStruct((db, n_new, width), BF16),
        compiler_params=_cparams(("parallel", "arbitrary")),
        name="attn_sample",
    )(page_table.reshape(-1), qbd, kn, vn, *([cache_k] * pp), *([cache_v] * pp), *lams, g)


def _route(logits):
    lane = lax.broadcasted_iota(I32, logits.shape, 1)
    lane_f = lane.astype(F32)
    big = float(LANES)
    is_grp = lane < N_GROUPS
    lg = jnp.where(is_grp, logits, -jnp.inf)
    mg = jnp.max(lg, axis=-1, keepdims=True)
    g_sel = jnp.min(jnp.where(lg == mg, lane_f, big), axis=-1, keepdims=True)
    p_grp = 1.0 / jnp.sum(jnp.exp(lg - mg), axis=-1, keepdims=True)
    lo = N_GROUPS + EXPERTS_PER_GROUP * g_sel
    le = jnp.where((lane_f >= lo) & (lane_f < lo + EXPERTS_PER_GROUP), logits, -jnp.inf)
    m1 = jnp.max(le, axis=-1, keepdims=True)
    i1 = jnp.min(jnp.where(le == m1, lane_f, big), axis=-1, keepdims=True)
    le2 = jnp.where(lane_f == i1, -jnp.inf, le)
    m2 = jnp.max(le2, axis=-1, keepdims=True)
    i2 = jnp.min(jnp.where(le2 == m2, lane_f, big), axis=-1, keepdims=True)
    e2 = jnp.exp(m2 - m1)
    w1 = p_grp / (1.0 + e2)
    w2 = p_grp * e2 / (1.0 + e2)
    wts = jnp.where(lane == 0, w1, jnp.where(lane == 1, w2, 0.0))
    eid = jnp.where(lane == 0, i1 - N_GROUPS, jnp.where(lane == 1, i2 - N_GROUPS, 0.0)).astype(I32)
    return wts, eid


def _merge_kernel(x_ref, o_ref, conv_ref, sgc_ref, sga_ref, wa_ref, wo_ref, gf_ref, wr_ref, br_ref,
                  x1_ref, h2_ref, wts_ref, eid_ref):
    attn = jnp.dot(o_ref[...], wa_ref[...], preferred_element_type=F32)
    merged = sgc_ref[...].astype(F32) * conv_ref[...] + sga_ref[...].astype(F32) * attn
    x1 = x_ref[...] + jnp.dot(merged.astype(BF16), wo_ref[...], preferred_element_type=F32)
    x1_ref[...] = x1
    inv = lax.rsqrt(jnp.mean(x1 * x1, axis=-1, keepdims=True) + RMS_EPS)
    h2 = x1 * inv * gf_ref[...]
    h2_ref[...] = h2
    logits = jnp.dot(h2.astype(BF16), wr_ref[...], preferred_element_type=F32) + br_ref[...]
    wts, eid = _route(logits)
    wts_ref[...] = wts
    eid_ref[...] = eid


def _merge(x, o, conv, sgc, sga, wa, wo, gf, wr, br, tm=256):
    t, d = x.shape
    tm = min(tm, t)
    row = lambda w: pl.BlockSpec((tm, w), lambda i: (i, 0))
    single = lambda shape: pl.BlockSpec(shape, lambda i: (0,) * len(shape), pipeline_mode=pl.Buffered(1))
    return pl.pallas_call(
        _merge_kernel,
        grid=(t // tm,),
        in_specs=[row(d), row(d), row(d), row(d), row(d), single((d, d)), single((d, d)),
                  _full((1, d)), _full((d, LANES)), _full((1, LANES))],
        out_specs=[row(d), row(d), row(LANES), row(LANES)],
        out_shape=[jax.ShapeDtypeStruct((t, d), F32), jax.ShapeDtypeStruct((t, d), F32),
                   jax.ShapeDtypeStruct((t, LANES), F32), jax.ShapeDtypeStruct((t, LANES), I32)],
        compiler_params=_cparams(("parallel",)),
        name="merge_route",
    )(x, o, conv, sgc, sga, wa, wo, gf, wr, br)


def _rank_kernel(eid_ref, rank_ref, cnt_ref, carry_ref, *, tr):
    @pl.when(pl.program_id(0) == 0)
    def _():
        carry_ref[...] = jnp.zeros(carry_ref.shape, F32)

    eid = eid_ref[...]
    lane = lax.broadcasted_iota(I32, eid.shape, 1)
    e1 = jnp.sum(jnp.where(lane == 0, eid, 0), axis=-1, keepdims=True)
    e2 = jnp.sum(jnp.where(lane == 1, eid, 0), axis=-1, keepdims=True)
    hit1 = lane == e1
    hit2 = lane == e2
    onehot = jnp.where(hit1 | hit2, 1.0, 0.0)
    r = lax.broadcasted_iota(I32, (tr, tr), 0)
    c = lax.broadcasted_iota(I32, (tr, tr), 1)
    below = jnp.where(c < r, 1.0, 0.0).astype(BF16)
    carry = carry_ref[0:1, :]
    prefix = jnp.dot(below, onehot.astype(BF16), preferred_element_type=F32) + carry
    r1 = jnp.sum(jnp.where(hit1, prefix, 0.0), axis=-1, keepdims=True)
    r2 = jnp.sum(jnp.where(hit2, prefix, 0.0), axis=-1, keepdims=True)
    rank_ref[...] = jnp.where(lane == 0, r1, jnp.where(lane == 1, r2, 0.0)).astype(I32)
    total = carry + jnp.sum(onehot, axis=0, keepdims=True)
    carry_ref[...] = jnp.broadcast_to(total, carry_ref.shape)
    cnt_ref[...] = jnp.broadcast_to(total, cnt_ref.shape)


def _rank(eid, tr):
    t = eid.shape[0]
    return pl.pallas_call(
        functools.partial(_rank_kernel, tr=tr),
        grid=(t // tr,),
        in_specs=[pl.BlockSpec((tr, LANES), lambda i: (i, 0))],
        out_specs=[pl.BlockSpec((tr, LANES), lambda i: (i, 0)), _full((8, LANES))],
        out_shape=[jax.ShapeDtypeStruct((t, LANES), I32), jax.ShapeDtypeStruct((8, LANES), F32)],
        scratch_shapes=[pltpu.VMEM((8, LANES), F32)],
        compiler_params=_cparams(("arbitrary",)),
        name="expert_rank",
    )(eid)


def _dispatch_kernel(pos_ref, h_ref, xs_in_ref, xs_ref, sem, *, td):
    del xs_in_ref

    def row_copy(r, k):
        p = pos_ref[0, 0, 2 * r + k]
        return pltpu.make_async_copy(h_ref.at[pl.ds(r, 1)], xs_ref.at[pl.ds(p, 1)], sem)

    def start(r, carry):
        row_copy(r, 0).start()
        row_copy(r, 1).start()
        return carry

    def wait(r, carry):
        row_copy(r, 0).wait()
        row_copy(r, 1).wait()
        return carry

    lax.fori_loop(0, td, start, 0)
    lax.fori_loop(0, td, wait, 0)


def _dispatch(h2, pos, n_rows, td):
    t, d = h2.shape
    pos3 = pos.reshape(t // td, 1, 2 * td)
    xs0 = jnp.zeros((n_rows, d), F32)
    return pl.pallas_call(
        functools.partial(_dispatch_kernel, td=td),
        grid=(t // td,),
        in_specs=[pl.BlockSpec((1, 1, 2 * td), lambda i: (i, 0, 0), memory_space=pltpu.SMEM),
                  pl.BlockSpec((td, d), lambda i: (i, 0)),
                  pl.BlockSpec(memory_space=pl.ANY)],
        out_specs=pl.BlockSpec(memory_space=pl.ANY),
        out_shape=jax.ShapeDtypeStruct((n_rows, d), F32),
        scratch_shapes=[pltpu.SemaphoreType.DMA(())],
        input_output_aliases={2: 0},
        compiler_params=_cparams(("arbitrary",)),
        name="moe_dispatch",
    )(pos3, h2, xs0)


def _experts_kernel(te_ref, nu_ref, x_ref, wg_ref, wu_ref, wd_ref, y_ref):
    del te_ref

    @pl.when(pl.program_id(0) < nu_ref[0])
    def _():
        x = x_ref[...].astype(BF16)
        hg = jnp.dot(x, wg_ref[...], preferred_element_type=F32)
        hu = jnp.dot(x, wu_ref[...], preferred_element_type=F32)
        act = hg * jax.nn.sigmoid(hg) * hu
        y_ref[...] = jnp.dot(act.astype(BF16), wd_ref[...], preferred_element_type=F32)

    @pl.when(pl.program_id(0) >= nu_ref[0])
    def _():
        y_ref[...] = jnp.zeros(y_ref.shape, F32)


def _experts(xs, tile_expert, n_used, wg, wu, wd, tme):
    n_rows, d = xs.shape
    f = wg.shape[2]
    n_tiles = n_rows // tme
    tile = lambda i, te, nu: (jnp.minimum(i, nu[0] - 1), 0)
    grid_spec = pltpu.PrefetchScalarGridSpec(
        num_scalar_prefetch=2,
        grid=(n_tiles,),
        in_specs=[pl.BlockSpec((tme, d), tile),
                  pl.BlockSpec((None, d, f), lambda i, te, nu: (te[i], 0, 0)),
                  pl.BlockSpec((None, d, f), lambda i, te, nu: (te[i], 0, 0)),
                  pl.BlockSpec((None, f, d), lambda i, te, nu: (te[i], 0, 0))],
        out_specs=pl.BlockSpec((tme, d), lambda i, te, nu: (i, 0)),
    )
    return pl.pallas_call(
        _experts_kernel,
        grid_spec=grid_spec,
        out_shape=jax.ShapeDtypeStruct((n_rows, d), F32),
        compiler_params=_cparams(("arbitrary",)),
        name="moe_experts",
    )(tile_expert, n_used, xs, wg, wu, wd)


def _combine_kernel(pos_ref, x1_ref, wts_ref, gfin_ref, y_ref, o_ref, buf_ref, sem, *, tc):
    def row_copy(r, k):
        p = pos_ref[0, 0, 2 * r + k]
        return pltpu.make_async_copy(y_ref.at[pl.ds(p, 1)], buf_ref.at[k, pl.ds(r, 1)], sem)

    def start(r, carry):
        row_copy(r, 0).start()
        row_copy(r, 1).start()
        return carry

    def wait(r, carry):
        row_copy(r, 0).wait()
        row_copy(r, 1).wait()
        return carry

    lax.fori_loop(0, tc, start, 0)
    lax.fori_loop(0, tc, wait, 0)
    wts = wts_ref[...]
    x2 = x1_ref[...] + wts[:, 0:1] * buf_ref[0] + wts[:, 1:2] * buf_ref[1]
    inv = lax.rsqrt(jnp.mean(x2 * x2, axis=-1, keepdims=True) + RMS_EPS)
    o_ref[...] = x2 * inv * gfin_ref[...]


def _combine(x1, wts, pos, y, gfin, tc):
    t, d = x1.shape
    pos3 = pos.reshape(t // tc, 1, 2 * tc)
    return pl.pallas_call(
        functools.partial(_combine_kernel, tc=tc),
        grid=(t // tc,),
        in_specs=[pl.BlockSpec((1, 1, 2 * tc), lambda i: (i, 0, 0), memory_space=pltpu.SMEM),
                  pl.BlockSpec((tc, d), lambda i: (i, 0)),
                  pl.BlockSpec((tc, LANES), lambda i: (i, 0)),
                  _full((1, d)),
                  pl.BlockSpec(memory_space=pl.ANY)],
        out_specs=pl.BlockSpec((tc, d), lambda i: (i, 0)),
        out_shape=jax.ShapeDtypeStruct((t, d), F32),
        scratch_shapes=[pltpu.VMEM((2, tc, d), F32), pltpu.SemaphoreType.DMA(())],
        compiler_params=_cparams(("arbitrary",)),
        name="moe_combine",
    )(pos3, x1, wts, gfin, y)


def _moe(x1, h2, wts, eid, wg, wu, wd, gfin, tme, tr, td):
    t = x1.shape[0]
    td = min(td, t)
    rank, cnt = _rank(eid, tr)
    counts = cnt[0, :N_EXPERTS].astype(I32)
    padded = ((counts + tme - 1) // tme) * tme
    ends = jnp.cumsum(padded)
    offsets = ends - padded
    pos = (offsets[eid[:, :2]] + rank[:, :2]).astype(I32)
    n_tiles = (2 * t + N_EXPERTS * (tme - 1)) // tme
    tile_expert = jnp.minimum(
        jnp.searchsorted(ends, jnp.arange(n_tiles, dtype=I32) * tme, side="right"), N_EXPERTS - 1).astype(I32)
    n_used = (ends[-1:] // tme).astype(I32)
    xs = _dispatch(h2, pos, n_tiles * tme, td)
    y = _experts(xs, tile_expert, n_used, wg, wu, wd, tme)
    return _combine(x1, wts, pos, y, gfin, td)


def kernel(x_prompt, x_sample, cache_k, cache_v, state_conv, page_table, norm_mix_g, w_in, conv_dw_w, conv_dw_b, conv_ln_g, conv_ln_b, conv_w_out, lambda_q1, lambda_k1, lambda_q2, lambda_k2, head_norm_g, attn_w_out, w_out, norm_ffn_g, router_group_w, router_group_b, router_expert_w, router_expert_b, expert_w_gate, expert_w_up, expert_w_down, final_norm_g):
    bsz, seq, d = x_prompt.shape
    db, ds, _ = x_sample.shape
    l = LAYER
    qkw = 2 * N_HEADS * HEAD_DIM
    vw = N_HEADS * V_DIM

    win = w_in[l]
    bounds = [0, D_CONV, 2 * D_CONV, 2 * D_CONV + qkw, 2 * D_CONV + 2 * qkw, 2 * D_CONV + 2 * qkw + vw,
              2 * D_CONV + 2 * qkw + vw + d, 2 * D_CONV + 2 * qkw + vw + 2 * d]
    w_a, w_b, w_q, w_k, w_v, w_gc, w_ga = [win[:, bounds[i]:bounds[i + 1]].astype(BF16) for i in range(7)]
    w_pw = conv_w_out[l].astype(BF16)
    w_ao = attn_w_out[l].astype(BF16)
    w_o = w_out[l].astype(BF16)
    w_r = jnp.zeros((d, LANES), F32).at[:, :N_GROUPS].set(router_group_w[l])
    w_r = w_r.at[:, N_GROUPS:N_GROUPS + N_EXPERTS].set(router_expert_w[l]).astype(BF16)
    b_r = jnp.zeros((1, LANES), F32).at[0, :N_GROUPS].set(router_group_b[l])
    b_r = b_r.at[0, N_GROUPS:N_GROUPS + N_EXPERTS].set(router_expert_b[l])
    wg = expert_w_gate[l].astype(BF16)
    wu = expert_w_up[l].astype(BF16)
    wd = expert_w_down[l].astype(BF16)
    row = lambda a: a.reshape(1, -1)
    g_mix, g_ffn, g_fin = row(norm_mix_g[l]), row(norm_ffn_g[l]), row(final_norm_g)
    dww, dwb = conv_dw_w[l], row(conv_dw_b[l])
    lng, lnb = row(conv_ln_g[l]), row(conv_ln_b[l])
    lams = [row(lambda_q1[l]), row(lambda_k1[l]), row(lambda_q2[l]), row(lambda_k2[l])]
    g_head = row(head_norm_g[l])

    def project(x2d, tm):
        (u,) = _norm_matmul(x2d, g_mix, [w_a, w_b], _glu_epilogue, [F32], tm, 256, "proj_glu")
        q, k, kb, v, vb = _norm_matmul(x2d, g_mix, [w_q, w_k, w_v], _qkv_epilogue,
                                       [BF16, F32, BF16, F32, BF16], tm, 256, "proj_qkv")
        sgc, sga = _norm_matmul(x2d, g_mix, [w_gc, w_ga], _gate_epilogue, [BF16, BF16], tm, 256, "proj_gates")
        return u, q, k, kb, v, vb, sgc, sga

    def finish(x2d, o, conv, sgc, sga, tme, tr, td):
        x1, h2, wts, eid = _merge(x2d, o, conv, sgc, sga, w_ao, w_o, g_ffn, w_r, b_r)
        return _moe(x1, h2, wts, eid, wg, wu, wd, g_fin, tme, tr, td)

    tp = bsz * seq
    xp = x_prompt.reshape(tp, d)
    u_p, q_p, k_p, kb_p, v_p, vb_p, sgc_p, sga_p = project(xp, 1024)
    conv_p = _conv_prompt(u_p.reshape(bsz, seq, D_CONV), dww, dwb, lng, lnb, w_pw).reshape(tp, d)
    o_p = _attn_prompt(q_p.reshape(bsz, seq, qkw), kb_p.reshape(bsz, seq, qkw), vb_p.reshape(bsz, seq, vw),
                       lams, g_head).reshape(tp, vw)
    y_p = finish(xp, o_p, conv_p, sgc_p, sga_p, 256, 512, 256).reshape(bsz, seq, d)

    tsamp = db * ds
    xs_ = x_sample.reshape(tsamp, d)
    u_s, q_s, k_s, kb_s, v_s, vb_s, sgc_s, sga_s = project(xs_, tsamp)
    u_ext = jnp.concatenate([state_conv[l], u_s.reshape(db, ds, D_CONV)], axis=1)
    conv_s = _conv_sample(jnp.swapaxes(u_ext, 0, 1), dww, dwb, lng, lnb, w_pw)
    conv_s = jnp.swapaxes(conv_s, 0, 1).reshape(tsamp, d)
    q5 = q_s.reshape(db, ds, N_HEADS, 2, HEAD_DIM).transpose(0, 2, 3, 1, 4)
    zero = jnp.zeros_like(q5[:, :, 0])
    qbd = jnp.stack([jnp.concatenate([q5[:, :, 0], zero], axis=-1),
                     jnp.concatenate([zero, q5[:, :, 1]], axis=-1)], axis=2).reshape(db, N_HEADS, 2 * ds, V_DIM)
    pad_rows = 16 - ds
    kn = jnp.pad(kb_s.reshape(db, ds, qkw), ((0, 0), (0, pad_rows), (0, 0)))
    vn = jnp.pad(vb_s.reshape(db, ds, vw), ((0, 0), (0, pad_rows), (0, 0)))
    n_pool = cache_k.shape[1]
    o_s = _attn_sample(qbd, kn, vn, cache_k[l].reshape(n_pool, PAGE_SIZE, qkw),
                       cache_v[l].reshape(n_pool, PAGE_SIZE, vw), page_table, lams, g_head).reshape(tsamp, vw)
    y_s = finish(xs_, o_s, conv_s, sgc_s, sga_s, 64, tsamp, 256).reshape(db, ds, d)

    k_prompt = k_p.reshape(1, bsz, seq, N_HEADS, 2 * HEAD_DIM)
    v_prompt = v_p.reshape(1, bsz, seq, N_HEADS, V_DIM)
    conv_prompt = u_p.reshape(bsz, seq, D_CONV)[:, seq - (CONV_W - 1):][None]
    k_sample = k_s.reshape(1, db, ds, N_HEADS, 2 * HEAD_DIM)
    v_sample = v_s.reshape(1, db, ds, N_HEADS, V_DIM)
    conv_sample = u_ext[:, ds:][None]
    return (y_p, y_s, k_prompt, v_prompt, conv_prompt, k_sample, v_sample, conv_sample)
```

```python
import functools
import math

import jax
import jax.numpy as jnp
from jax import lax
from jax.experimental import pallas as pl
from jax.experimental.pallas import tpu as pltpu

F32 = jnp.float32
BF16 = jnp.bfloat16
I32 = jnp.int32

D_MODEL = 2048
HEAD_DIM = 128
N_HEADS = D_MODEL // (2 * HEAD_DIM)
V_DIM = 2 * HEAD_DIM
D_CONV = D_MODEL // 2
CONV_W = 31
N_GROUPS = 4
EXPERTS_PER_GROUP = 8
N_EXPERTS = N_GROUPS * EXPERTS_PER_GROUP
D_EXPERT = D_MODEL // 4
PAGE_SIZE = 128
RMS_EPS = 1e-6
LN_EPS = 1e-5
ALIBI_MAX_EXP = 8.0
LAYER = 0
LAM_INIT = 0.8 - 0.6 * math.exp(-0.3 * LAYER)
LOG2E = math.log2(math.e)

LANES = 128
SUBLANES = 8
NEG_BIG = -1e30
VMEM_LIMIT = 56 * 1024 * 1024


def _cparams(sem):
    return pltpu.CompilerParams(dimension_semantics=sem, vmem_limit_bytes=VMEM_LIMIT)


def _full(shape):
    return pl.BlockSpec(shape, lambda *_: (0,) * len(shape))


def _norm_matmul_kernel(x_ref, g_ref, *refs, n_w, n_out, epilogue):
    w_refs = refs[:n_w]
    o_refs = refs[n_w:n_w + n_out]
    h_ref = refs[n_w + n_out]

    @pl.when(pl.program_id(1) == 0)
    def _():
        x = x_ref[...]
        inv = lax.rsqrt(jnp.mean(x * x, axis=-1, keepdims=True) + RMS_EPS)
        h_ref[...] = (x * inv * g_ref[...]).astype(BF16)

    h = h_ref[...]
    zs = [jnp.dot(h, w[...], preferred_element_type=F32) for w in w_refs]
    for o_ref, val in zip(o_refs, epilogue(*zs)):
        o_ref[...] = val.astype(o_ref.dtype)


def _norm_matmul(x, g, ws, epilogue, out_dtypes, tm, tn, name):
    t, d = x.shape
    n = ws[0].shape[1]
    kern = functools.partial(_norm_matmul_kernel, n_w=len(ws), n_out=len(out_dtypes), epilogue=epilogue)
    return pl.pallas_call(
        kern,
        grid=(t // tm, n // tn),
        in_specs=[pl.BlockSpec((tm, d), lambda i, j: (i, 0)), _full((1, d))]
        + [pl.BlockSpec((d, tn), lambda i, j: (0, j)) for _ in ws],
        out_specs=[pl.BlockSpec((tm, tn), lambda i, j: (i, j)) for _ in out_dtypes],
        out_shape=[jax.ShapeDtypeStruct((t, n), dt) for dt in out_dtypes],
        scratch_shapes=[pltpu.VMEM((tm, d), BF16)],
        compiler_params=_cparams(("parallel", "arbitrary")),
        name=name,
    )(x, g, *ws)


def _glu_epilogue(a, b):
    return (a * jax.nn.sigmoid(b),)


def _qkv_epilogue(q, k, v):
    return (q * (LOG2E * HEAD_DIM ** -0.5), k, k, v, v)


def _gate_epilogue(gc, ga):
    return (jax.nn.sigmoid(gc), jax.nn.sigmoid(ga))


def _ln_swish_project(z, lng_ref, lnb_ref, wpw_ref):
    mu = jnp.mean(z, axis=-1, keepdims=True)
    zc = z - mu
    var = jnp.mean(zc * zc, axis=-1, keepdims=True)
    y = zc * lax.rsqrt(var + LN_EPS) * lng_ref[...] + lnb_ref[...]
    y = y * jax.nn.sigmoid(y)
    return jnp.dot(y.astype(BF16), wpw_ref[...], preferred_element_type=F32)


HALO = 32
CONV_ROWS = 128


def _conv_prompt_kernel(u_ref, halo_ref, dww_ref, dwb_ref, lng_ref, lnb_ref, wpw_ref, o_ref, ext_ref, z_ref, sh_ref,
                        *, ts):
    first = pl.program_id(1) == 0
    ext_ref[0:HALO, :] = jnp.where(first, 0.0, halo_ref[...])
    ext_ref[HALO:, :] = u_ref[...]
    base = HALO - (CONV_W - 1)
    for cb in range(D_CONV // LANES):
        cs = slice(cb * LANES, (cb + 1) * LANES)
        for shift in range(SUBLANES):
            span = ts + HALO - (SUBLANES if shift else 0)
            sh_ref[shift, 0:span, :] = ext_ref[shift:shift + span, cs]
        for r0 in range(0, ts, CONV_ROWS):
            acc = jnp.broadcast_to(dwb_ref[:, cs], (CONV_ROWS, LANES))
            for j in range(CONV_W):
                shift = (base + j) % SUBLANES
                a = base + j - shift + r0
                acc = acc + dww_ref[j:j + 1, cs] * sh_ref[shift, a:a + CONV_ROWS, :]
            z_ref[r0:r0 + CONV_ROWS, cs] = acc
    o_ref[...] = _ln_swish_project(z_ref[...], lng_ref, lnb_ref, wpw_ref)


def _conv_prompt(u, dww, dwb, lng, lnb, wpw, ts=256):
    b, s, c = u.shape
    d = wpw.shape[1]
    r = ts // HALO
    return pl.pallas_call(
        functools.partial(_conv_prompt_kernel, ts=ts),
        grid=(b, s // ts),
        in_specs=[
            pl.BlockSpec((None, ts, c), lambda bi, i: (bi, i, 0)),
            pl.BlockSpec((None, HALO, c), lambda bi, i: (bi, jnp.maximum(i * r - 1, 0), 0)),
            _full((CONV_W, c)), _full((1, c)), _full((1, c)), _full((1, c)), _full((c, d)),
        ],
        out_specs=pl.BlockSpec((None, ts, d), lambda bi, i: (bi, i, 0)),
        out_shape=jax.ShapeDtypeStruct((b, s, d), F32),
        scratch_shapes=[pltpu.VMEM((ts + HALO, c), F32), pltpu.VMEM((ts, c), F32),
                        pltpu.VMEM((SUBLANES, ts + HALO, LANES), F32)],
        compiler_params=_cparams(("parallel", "arbitrary")),
        name="conv_prompt",
    )(u, u, dww, dwb, lng, lnb, wpw)


def _conv_sample_kernel(ext_ref, dww_ref, dwb_ref, lng_ref, lnb_ref, wpw_ref, o_ref, *, n_new, bt):
    zs = []
    for t in range(n_new):
        acc = jnp.broadcast_to(dwb_ref[...], (bt, D_CONV))
        for j in range(CONV_W):
            acc = acc + dww_ref[j:j + 1, :] * ext_ref[t + j]
        zs.append(acc)
    out = _ln_swish_project(jnp.concatenate(zs, axis=0), lng_ref, lnb_ref, wpw_ref)
    for t in range(n_new):
        o_ref[t] = out[t * bt:(t + 1) * bt]


def _conv_sample(ext_t, dww, dwb, lng, lnb, wpw, bt=32):
    rows, b, c = ext_t.shape
    n_new = rows - (CONV_W - 1)
    d = wpw.shape[1]
    return pl.pallas_call(
        functools.partial(_conv_sample_kernel, n_new=n_new, bt=bt),
        grid=(b // bt,),
        in_specs=[
            pl.BlockSpec((rows, bt, c), lambda i: (0, i, 0)),
            _full((CONV_W, c)), _full((1, c)), _full((1, c)), _full((1, c)), _full((c, d)),
        ],
        out_specs=pl.BlockSpec((n_new, bt, d), lambda i: (0, i, 0)),
        out_shape=jax.ShapeDtypeStruct((n_new, b, d), F32),
        compiler_params=_cparams(("parallel",)),
        name="conv_sample",
    )(ext_t, dww, dwb, lng, lnb, wpw)


def _lambda_value(lq1_ref, lk1_ref, lq2_ref, lk2_ref):
    a = jnp.sum(lq1_ref[...] * lk1_ref[...], axis=-1, keepdims=True)
    b = jnp.sum(lq2_ref[...] * lk2_ref[...], axis=-1, keepdims=True)
    return jnp.exp(a) - jnp.exp(b) + LAM_INIT


def _head_norm(o, g_ref):
    inv = lax.rsqrt(jnp.mean(o * o, axis=-1, keepdims=True) + RMS_EPS)
    return o * inv * g_ref[...] * (1.0 - LAM_INIT)


def _alibi_slope(h, shape):
    return jnp.exp2(jnp.full(shape, -(ALIBI_MAX_EXP / N_HEADS), F32) * (h + 1).astype(F32))


def _nt_dot(a, b):
    return lax.dot_general(a, b, (((1,), (1,)), ((), ())), preferred_element_type=F32)


def _attn_prompt_kernel(q_ref, k_ref, v_ref, lq1_ref, lk1_ref, lq2_ref, lk2_ref, g_ref, o_ref, *, tq, seq):
    h = pl.program_id(1)
    kpos = lax.broadcasted_iota(I32, (1, seq), 1).astype(F32)
    bias = _alibi_slope(h, (1, seq)) * LOG2E * kpos
    lam = _lambda_value(lq1_ref, lk1_ref, lq2_ref, lk2_ref)
    visible = lax.broadcasted_iota(I32, (tq, tq), 1) <= lax.broadcasted_iota(I32, (tq, tq), 0)
    for i in range(seq // tq):
        lo, hi = i * tq, (i + 1) * tq
        outs = []
        for c in range(2):
            cs = slice(c * HEAD_DIM, (c + 1) * HEAD_DIM)
            q = q_ref[lo:hi, cs]
            s_d = jnp.where(visible, _nt_dot(q, k_ref[lo:hi, cs]) + bias[:, lo:hi], -jnp.inf)
            m = jnp.max(s_d, axis=-1, keepdims=True)
            if i > 0:
                s_o = _nt_dot(q, k_ref[0:lo, cs]) + bias[:, 0:lo]
                m = jnp.maximum(m, jnp.max(s_o, axis=-1, keepdims=True))
            p_d = jnp.exp2(s_d - m)
            l = jnp.sum(p_d, axis=-1, keepdims=True)
            pv = jnp.dot(p_d.astype(BF16), v_ref[lo:hi, :], preferred_element_type=F32)
            if i > 0:
                p_o = jnp.exp2(s_o - m)
                l = l + jnp.sum(p_o, axis=-1, keepdims=True)
                pv = pv + jnp.dot(p_o.astype(BF16), v_ref[0:lo, :], preferred_element_type=F32)
            outs.append(pv / l)
        o_ref[lo:hi, :] = _head_norm(outs[0] - lam * outs[1], g_ref).astype(o_ref.dtype)


def _attn_prompt(q, k, v, lams, g, tq=512):
    b, s, _ = q.shape
    w = 2 * HEAD_DIM
    spec = pl.BlockSpec((None, s, w), lambda bi, h: (bi, 0, h))
    return pl.pallas_call(
        functools.partial(_attn_prompt_kernel, tq=min(tq, s), seq=s),
        grid=(b, N_HEADS),
        in_specs=[spec, spec, spec] + [_full((1, HEAD_DIM))] * 4 + [_full((1, V_DIM))],
        out_specs=pl.BlockSpec((None, s, V_DIM), lambda bi, h: (bi, 0, h)),
        out_shape=jax.ShapeDtypeStruct((b, s, N_HEADS * V_DIM), BF16),
        compiler_params=_cparams(("parallel", "parallel")),
        name="attn_prompt",
    )(q, k, v, *lams, g)


NEW_PAD = 128


def _attn_sample_kernel(pt_ref, q_ref, kn_ref, vn_ref, bias_ref, off_ref, biasn_ref, *refs, pp, n_new, n_steps):
    k_refs = refs[:pp]
    v_refs = refs[pp:2 * pp]
    lq1_ref, lk1_ref, lq2_ref, lk2_ref, g_ref, o_ref, kb_ref, vb_ref, m_ref, l_ref, acc_ref = refs[2 * pp:]
    step = pl.program_id(1)
    pr = PAGE_SIZE * N_HEADS
    q = q_ref[...]

    @pl.when(step == 0)
    def _():
        m_ref[...] = jnp.full(m_ref.shape, NEG_BIG, F32)
        l_ref[...] = jnp.zeros(l_ref.shape, F32)
        acc_ref[...] = jnp.zeros(acc_ref.shape, F32)

    def update(s, off, v):
        m_prev = m_ref[:, :1]
        m_new = jnp.maximum(m_prev, jnp.max(s, axis=-1, keepdims=True) + off)
        alpha = jnp.exp2(m_prev - m_new)
        p = jnp.exp2(s - (m_new - off))
        l_new = alpha * l_ref[:, :1] + jnp.sum(p, axis=-1, keepdims=True)
        acc_ref[...] = alpha * acc_ref[...] + jnp.dot(p.astype(BF16), v, preferred_element_type=F32)
        m_ref[...] = jnp.broadcast_to(m_new, m_ref.shape)
        l_ref[...] = jnp.broadcast_to(l_new, l_ref.shape)

    for j in range(pp):
        kb_ref[j * pr:(j + 1) * pr, :] = k_refs[j][...].astype(BF16)
        vb_ref[j * pr:(j + 1) * pr, :] = v_refs[j][...].astype(BF16)
    off = off_ref[:, :1] * step.astype(F32)
    update(_nt_dot(q, kb_ref[...]) + bias_ref[...], off, vb_ref[...])

    @pl.when(step == n_steps - 1)
    def _():
        update(_nt_dot(q, kn_ref[...]) + biasn_ref[...], 0.0, vn_ref[...])
        lam = _lambda_value(lq1_ref, lk1_ref, lq2_ref, lk2_ref)
        a = acc_ref[...] / l_ref[:, :1]
        for h in range(N_HEADS):
            r0 = h * 2 * n_new
            o = a[r0:r0 + n_new] - lam * a[r0 + n_new:r0 + 2 * n_new]
            o_ref[:, h * V_DIM:(h + 1) * V_DIM] = _head_norm(o, g_ref).astype(o_ref.dtype)


def _attn_sample(q_all, kn, vn, cache_k, cache_v, page_table, lams, g, pp=4):
    db, n_pages = page_table.shape
    rows = q_all.shape[1]
    n_new = rows // (2 * N_HEADS)
    n_steps = n_pages // pp
    pr = PAGE_SIZE * N_HEADS
    nk = pp * pr

    slopes = 2.0 ** (-(ALIBI_MAX_EXP / N_HEADS) * (jnp.arange(rows, dtype=F32) // (2 * n_new) + 1.0))[:, None]
    row_head = (jnp.arange(rows, dtype=I32) // (2 * n_new))[:, None]
    row_q = (jnp.arange(rows, dtype=I32) % n_new)[:, None]
    col = jnp.arange(nk, dtype=I32)[None, :]
    bias = jnp.where(col % N_HEADS == row_head, slopes * LOG2E * (col // N_HEADS).astype(F32), -jnp.inf)
    off = jnp.broadcast_to(slopes * LOG2E * (pp * PAGE_SIZE), (rows, LANES))
    coln = jnp.arange(NEW_PAD, dtype=I32)[None, :]
    new_ok = (coln % N_HEADS == row_head) & (coln // N_HEADS <= row_q)
    biasn = jnp.where(new_ok, slopes * LOG2E * (n_pages * PAGE_SIZE + coln // N_HEADS).astype(F32), -jnp.inf)

    def page_spec(j):
        return pl.BlockSpec((None, pr, V_DIM), lambda b, s, pt: (pt[b * n_pages + s * pp + j], 0, 0))

    const = lambda shape: pl.BlockSpec(shape, lambda b, s, pt: (0,) * len(shape))
    per_seq = lambda r: pl.BlockSpec((None, r, V_DIM), lambda b, s, pt: (b, 0, 0))
    grid_spec = pltpu.PrefetchScalarGridSpec(
        num_scalar_prefetch=1,
        grid=(db, n_steps),
        in_specs=[per_seq(rows), per_seq(NEW_PAD), per_seq(NEW_PAD),
                  const((rows, nk)), const((rows, LANES)), const((rows, NEW_PAD))]
        + [page_spec(j) for j in range(pp)] + [page_spec(j) for j in range(pp)]
        + [const((1, HEAD_DIM))] * 4 + [const((1, V_DIM))],
        out_specs=pl.BlockSpec((None, n_new, N_HEADS * V_DIM), lambda b, s, pt: (b, 0, 0)),
        scratch_shapes=[pltpu.VMEM((nk, V_DIM), BF16), pltpu.VMEM((nk, V_DIM), BF16),
                        pltpu.VMEM((rows, LANES), F32), pltpu.VMEM((rows, LANES), F32),
                        pltpu.VMEM((rows, V_DIM), F32)],
    )
    return pl.pallas_call(
        functools.partial(_attn_sample_kernel, pp=pp, n_new=n_new, n_steps=n_steps),
        grid_spec=grid_spec,
        out_shape=jax.ShapeDtypeStruct((db, n_new, N_HEADS * V_DIM), BF16),
        compiler_params=_cparams(("parallel", "arbitrary")),
        name="attn_sample",
    )(page_table.reshape(-1), q_all, kn, vn, bias, off, biasn, *([cache_k] * pp), *([cache_v] * pp), *lams, g)


def _route(logits):
    lane = lax.broadcasted_iota(I32, logits.shape, 1)
    lane_f = lane.astype(F32)
    big = float(LANES)
    is_grp = lane < N_GROUPS
    lg = jnp.where(is_grp, logits, -jnp.inf)
    mg = jnp.max(lg, axis=-1, keepdims=True)
    g_sel = jnp.min(jnp.where(lg == mg, lane_f, big), axis=-1, keepdims=True)
    p_grp = 1.0 / jnp.sum(jnp.exp(lg - mg), axis=-1, keepdims=True)
    lo = N_GROUPS + EXPERTS_PER_GROUP * g_sel
    le = jnp.where((lane_f >= lo) & (lane_f < lo + EXPERTS_PER_GROUP), logits, -jnp.inf)
    m1 = jnp.max(le, axis=-1, keepdims=True)
    i1 = jnp.min(jnp.where(le == m1, lane_f, big), axis=-1, keepdims=True)
    le2 = jnp.where(lane_f == i1, -jnp.inf, le)
    m2 = jnp.max(le2, axis=-1, keepdims=True)
    i2 = jnp.min(jnp.where(le2 == m2, lane_f, big), axis=-1, keepdims=True)
    e2 = jnp.exp(m2 - m1)
    w1 = p_grp / (1.0 + e2)
    w2 = p_grp * e2 / (1.0 + e2)
    wts = jnp.where(lane == 0, w1, jnp.where(lane == 1, w2, 0.0))
    eid = jnp.where(lane == 0, i1 - N_GROUPS, jnp.where(lane == 1, i2 - N_GROUPS, 0.0)).astype(I32)
    return wts, eid


def _merge_kernel(x_ref, o_ref, conv_ref, sgc_ref, sga_ref, wa_ref, wo_ref, gf_ref, wr_ref, br_ref,
                  x1_ref, h2_ref, wts_ref, eid_ref):
    attn = jnp.dot(o_ref[...], wa_ref[...], preferred_element_type=F32)
    merged = sgc_ref[...].astype(F32) * conv_ref[...] + sga_ref[...].astype(F32) * attn
    x1 = x_ref[...] + jnp.dot(merged.astype(BF16), wo_ref[...], preferred_element_type=F32)
    x1_ref[...] = x1
    inv = lax.rsqrt(jnp.mean(x1 * x1, axis=-1, keepdims=True) + RMS_EPS)
    h2 = x1 * inv * gf_ref[...]
    h2_ref[...] = h2
    logits = jnp.dot(h2.astype(BF16), wr_ref[...], preferred_element_type=F32) + br_ref[...]
    wts, eid = _route(logits)
    wts_ref[...] = wts
    eid_ref[...] = eid


def _merge(x, o, conv, sgc, sga, wa, wo, gf, wr, br, tm=256):
    t, d = x.shape
    tm = min(tm, t)
    row = lambda w: pl.BlockSpec((tm, w), lambda i: (i, 0))
    single = lambda shape: pl.BlockSpec(shape, lambda i: (0,) * len(shape), pipeline_mode=pl.Buffered(1))
    return pl.pallas_call(
        _merge_kernel,
        grid=(t // tm,),
        in_specs=[row(d), row(d), row(d), row(d), row(d), single((d, d)), single((d, d)),
                  _full((1, d)), _full((d, LANES)), _full((1, LANES))],
        out_specs=[row(d), row(d), row(LANES), row(LANES)],
        out_shape=[jax.ShapeDtypeStruct((t, d), F32), jax.ShapeDtypeStruct((t, d), F32),
                   jax.ShapeDtypeStruct((t, LANES), F32), jax.ShapeDtypeStruct((t, LANES), I32)],
        compiler_params=_cparams(("parallel",)),
        name="merge_route",
    )(x, o, conv, sgc, sga, wa, wo, gf, wr, br)


def _rank_kernel(eid_ref, rank_ref, cnt_ref, carry_ref, *, tr):
    @pl.when(pl.program_id(0) == 0)
    def _():
        carry_ref[...] = jnp.zeros(carry_ref.shape, F32)

    eid = eid_ref[...]
    lane = lax.broadcasted_iota(I32, eid.shape, 1)
    e1 = jnp.sum(jnp.where(lane == 0, eid, 0), axis=-1, keepdims=True)
    e2 = jnp.sum(jnp.where(lane == 1, eid, 0), axis=-1, keepdims=True)
    hit1 = lane == e1
    hit2 = lane == e2
    onehot = jnp.where(hit1 | hit2, 1.0, 0.0)
    r = lax.broadcasted_iota(I32, (tr, tr), 0)
    c = lax.broadcasted_iota(I32, (tr, tr), 1)
    below = jnp.where(c < r, 1.0, 0.0).astype(BF16)
    carry = carry_ref[0:1, :]
    prefix = jnp.dot(below, onehot.astype(BF16), preferred_element_type=F32) + carry
    r1 = jnp.sum(jnp.where(hit1, prefix, 0.0), axis=-1, keepdims=True)
    r2 = jnp.sum(jnp.where(hit2, prefix, 0.0), axis=-1, keepdims=True)
    rank_ref[...] = jnp.where(lane == 0, r1, jnp.where(lane == 1, r2, 0.0)).astype(I32)
    total = carry + jnp.sum(onehot, axis=0, keepdims=True)
    carry_ref[...] = jnp.broadcast_to(total, carry_ref.shape)
    cnt_ref[...] = jnp.broadcast_to(total, cnt_ref.shape)


def _rank(eid, tr):
    t = eid.shape[0]
    return pl.pallas_call(
        functools.partial(_rank_kernel, tr=tr),
        grid=(t // tr,),
        in_specs=[pl.BlockSpec((tr, LANES), lambda i: (i, 0))],
        out_specs=[pl.BlockSpec((tr, LANES), lambda i: (i, 0)), _full((8, LANES))],
        out_shape=[jax.ShapeDtypeStruct((t, LANES), I32), jax.ShapeDtypeStruct((8, LANES), F32)],
        scratch_shapes=[pltpu.VMEM((8, LANES), F32)],
        compiler_params=_cparams(("arbitrary",)),
        name="expert_rank",
    )(eid)


def _dispatch_kernel(pos_ref, h_ref, xs_in_ref, xs_ref, sem, *, td):
    del xs_in_ref

    def row_copy(r, k):
        p = pos_ref[0, 0, 2 * r + k]
        return pltpu.make_async_copy(h_ref.at[pl.ds(r, 1)], xs_ref.at[pl.ds(p, 1)], sem)

    def start(r, carry):
        row_copy(r, 0).start()
        row_copy(r, 1).start()
        return carry

    def wait(r, carry):
        row_copy(r, 0).wait()
        row_copy(r, 1).wait()
        return carry

    lax.fori_loop(0, td, start, 0)
    lax.fori_loop(0, td, wait, 0)


def _dispatch(h2, pos, n_rows, td):
    t, d = h2.shape
    pos3 = pos.reshape(t // td, 1, 2 * td)
    xs0 = jnp.zeros((n_rows, d), F32)
    return pl.pallas_call(
        functools.partial(_dispatch_kernel, td=td),
        grid=(t // td,),
        in_specs=[pl.BlockSpec((1, 1, 2 * td), lambda i: (i, 0, 0), memory_space=pltpu.SMEM),
                  pl.BlockSpec((td, d), lambda i: (i, 0)),
                  pl.BlockSpec(memory_space=pl.ANY)],
        out_specs=pl.BlockSpec(memory_space=pl.ANY),
        out_shape=jax.ShapeDtypeStruct((n_rows, d), F32),
        scratch_shapes=[pltpu.SemaphoreType.DMA(())],
        input_output_aliases={2: 0},
        compiler_params=_cparams(("arbitrary",)),
        name="moe_dispatch",
    )(pos3, h2, xs0)


def _experts_kernel(te_ref, nu_ref, x_ref, wg_ref, wu_ref, wd_ref, y_ref, wgb_ref, wub_ref, wdb_ref):
    i = pl.program_id(0)
    active = i < nu_ref[0]
    new_expert = (i == 0) | (te_ref[i] != te_ref[jnp.maximum(i - 1, 0)])

    @pl.when(active & new_expert)
    def _():
        wgb_ref[...] = wg_ref[...].astype(BF16)
        wub_ref[...] = wu_ref[...].astype(BF16)
        wdb_ref[...] = wd_ref[...].astype(BF16)

    @pl.when(active)
    def _():
        x = x_ref[...].astype(BF16)
        hg = jnp.dot(x, wgb_ref[...], preferred_element_type=F32)
        hu = jnp.dot(x, wub_ref[...], preferred_element_type=F32)
        act = hg * jax.nn.sigmoid(hg) * hu
        y_ref[...] = jnp.dot(act.astype(BF16), wdb_ref[...], preferred_element_type=F32)

    @pl.when(pl.program_id(0) >= nu_ref[0])
    def _():
        y_ref[...] = jnp.zeros(y_ref.shape, F32)


def _experts(xs, tile_expert, n_used, wg, wu, wd, tme):
    n_rows, d = xs.shape
    f = wg.shape[2]
    n_tiles = n_rows // tme
    tile = lambda i, te, nu: (jnp.minimum(i, nu[0] - 1), 0)
    grid_spec = pltpu.PrefetchScalarGridSpec(
        num_scalar_prefetch=2,
        grid=(n_tiles,),
        in_specs=[pl.BlockSpec((tme, d), tile),
                  pl.BlockSpec((None, d, f), lambda i, te, nu: (te[i], 0, 0)),
                  pl.BlockSpec((None, d, f), lambda i, te, nu: (te[i], 0, 0)),
                  pl.BlockSpec((None, f, d), lambda i, te, nu: (te[i], 0, 0))],
        out_specs=pl.BlockSpec((tme, d), lambda i, te, nu: (i, 0)),
        scratch_shapes=[pltpu.VMEM((d, f), BF16), pltpu.VMEM((d, f), BF16), pltpu.VMEM((f, d), BF16)],
    )
    return pl.pallas_call(
        _experts_kernel,
        grid_spec=grid_spec,
        out_shape=jax.ShapeDtypeStruct((n_rows, d), F32),
        compiler_params=_cparams(("arbitrary",)),
        name="moe_experts",
    )(tile_expert, n_used, xs, wg, wu, wd)


def _combine_kernel(pos_ref, x1_ref, wts_ref, gfin_ref, y_ref, o_ref, buf_ref, sem, *, tc):
    def row_copy(r, k):
        p = pos_ref[0, 0, 2 * r + k]
        return pltpu.make_async_copy(y_ref.at[pl.ds(p, 1)], buf_ref.at[k, pl.ds(r, 1)], sem)

    def start(r, carry):
        row_copy(r, 0).start()
        row_copy(r, 1).start()
        return carry

    def wait(r, carry):
        row_copy(r, 0).wait()
        row_copy(r, 1).wait()
        return carry

    lax.fori_loop(0, tc, start, 0)
    lax.fori_loop(0, tc, wait, 0)
    wts = wts_ref[...]
    x2 = x1_ref[...] + wts[:, 0:1] * buf_ref[0] + wts[:, 1:2] * buf_ref[1]
    inv = lax.rsqrt(jnp.mean(x2 * x2, axis=-1, keepdims=True) + RMS_EPS)
    o_ref[...] = x2 * inv * gfin_ref[...]


def _combine(x1, wts, pos, y, gfin, tc):
    t, d = x1.shape
    pos3 = pos.reshape(t // tc, 1, 2 * tc)
    return pl.pallas_call(
        functools.partial(_combine_kernel, tc=tc),
        grid=(t // tc,),
        in_specs=[pl.BlockSpec((1, 1, 2 * tc), lambda i: (i, 0, 0), memory_space=pltpu.SMEM),
                  pl.BlockSpec((tc, d), lambda i: (i, 0)),
                  pl.BlockSpec((tc, LANES), lambda i: (i, 0)),
                  _full((1, d)),
                  pl.BlockSpec(memory_space=pl.ANY)],
        out_specs=pl.BlockSpec((tc, d), lambda i: (i, 0)),
        out_shape=jax.ShapeDtypeStruct((t, d), F32),
        scratch_shapes=[pltpu.VMEM((2, tc, d), F32), pltpu.SemaphoreType.DMA(())],
        compiler_params=_cparams(("arbitrary",)),
        name="moe_combine",
    )(pos3, x1, wts, gfin, y)


def _moe(x1, h2, wts, eid, wg, wu, wd, gfin, tme, tr, td):
    t = x1.shape[0]
    td = min(td, t)
    rank, cnt = _rank(eid, tr)
    counts = cnt[0, :N_EXPERTS].astype(I32)
    padded = ((counts + tme - 1) // tme) * tme
    ends = jnp.cumsum(padded)
    offsets = ends - padded
    pos = (offsets[eid[:, :2]] + rank[:, :2]).astype(I32)
    n_tiles = (2 * t + N_EXPERTS * (tme - 1)) // tme
    tile_expert = jnp.minimum(
        jnp.searchsorted(ends, jnp.arange(n_tiles, dtype=I32) * tme, side="right"), N_EXPERTS - 1).astype(I32)
    n_used = (ends[-1:] // tme).astype(I32)
    xs = _dispatch(h2, pos, n_tiles * tme, td)
    y = _experts(xs, tile_expert, n_used, wg, wu, wd, tme)
    return _combine(x1, wts, pos, y, gfin, td)


def kernel(x_prompt, x_sample, cache_k, cache_v, state_conv, page_table, norm_mix_g, w_in, conv_dw_w, conv_dw_b, conv_ln_g, conv_ln_b, conv_w_out, lambda_q1, lambda_k1, lambda_q2, lambda_k2, head_norm_g, attn_w_out, w_out, norm_ffn_g, router_group_w, router_group_b, router_expert_w, router_expert_b, expert_w_gate, expert_w_up, expert_w_down, final_norm_g):
    bsz, seq, d = x_prompt.shape
    db, ds, _ = x_sample.shape
    l = LAYER
    qkw = 2 * N_HEADS * HEAD_DIM
    vw = N_HEADS * V_DIM

    win = w_in[l]
    bounds = [0, D_CONV, 2 * D_CONV, 2 * D_CONV + qkw, 2 * D_CONV + 2 * qkw, 2 * D_CONV + 2 * qkw + vw,
              2 * D_CONV + 2 * qkw + vw + d, 2 * D_CONV + 2 * qkw + vw + 2 * d]
    w_a, w_b, w_q, w_k, w_v, w_gc, w_ga = [win[:, bounds[i]:bounds[i + 1]].astype(BF16) for i in range(7)]
    w_pw = conv_w_out[l].astype(BF16)
    w_ao = attn_w_out[l].astype(BF16)
    w_o = w_out[l].astype(BF16)
    w_r = jnp.zeros((d, LANES), F32).at[:, :N_GROUPS].set(router_group_w[l])
    w_r = w_r.at[:, N_GROUPS:N_GROUPS + N_EXPERTS].set(router_expert_w[l]).astype(BF16)
    b_r = jnp.zeros((1, LANES), F32).at[0, :N_GROUPS].set(router_group_b[l])
    b_r = b_r.at[0, N_GROUPS:N_GROUPS + N_EXPERTS].set(router_expert_b[l])
    wg, wu, wd = expert_w_gate[l], expert_w_up[l], expert_w_down[l]
    row = lambda a: a.reshape(1, -1)
    g_mix, g_ffn, g_fin = row(norm_mix_g[l]), row(norm_ffn_g[l]), row(final_norm_g)
    dww, dwb = conv_dw_w[l], row(conv_dw_b[l])
    lng, lnb = row(conv_ln_g[l]), row(conv_ln_b[l])
    lams = [row(lambda_q1[l]), row(lambda_k1[l]), row(lambda_q2[l]), row(lambda_k2[l])]
    g_head = row(head_norm_g[l])

    def project(x2d, tm):
        (u,) = _norm_matmul(x2d, g_mix, [w_a, w_b], _glu_epilogue, [F32], tm, 256, "proj_glu")
        q, k, kb, v, vb = _norm_matmul(x2d, g_mix, [w_q, w_k, w_v], _qkv_epilogue,
                                       [BF16, F32, BF16, F32, BF16], tm, 256, "proj_qkv")
        sgc, sga = _norm_matmul(x2d, g_mix, [w_gc, w_ga], _gate_epilogue, [BF16, BF16], tm, 256, "proj_gates")
        return u, q, k, kb, v, vb, sgc, sga

    def finish(x2d, o, conv, sgc, sga, tme, tr, td):
        x1, h2, wts, eid = _merge(x2d, o, conv, sgc, sga, w_ao, w_o, g_ffn, w_r, b_r)
        return _moe(x1, h2, wts, eid, wg, wu, wd, g_fin, tme, tr, td)

    tp = bsz * seq
    xp = x_prompt.reshape(tp, d)
    u_p, q_p, k_p, kb_p, v_p, vb_p, sgc_p, sga_p = project(xp, 1024)
    conv_p = _conv_prompt(u_p.reshape(bsz, seq, D_CONV), dww, dwb, lng, lnb, w_pw).reshape(tp, d)
    o_p = _attn_prompt(q_p.reshape(bsz, seq, qkw), kb_p.reshape(bsz, seq, qkw), vb_p.reshape(bsz, seq, vw),
                       lams, g_head).reshape(tp, vw)
    y_p = finish(xp, o_p, conv_p, sgc_p, sga_p, 256, 512, 256).reshape(bsz, seq, d)

    tsamp = db * ds
    xs_ = x_sample.reshape(tsamp, d)
    u_s, q_s, k_s, kb_s, v_s, vb_s, sgc_s, sga_s = project(xs_, tsamp)
    u_ext = jnp.concatenate([state_conv[l], u_s.reshape(db, ds, D_CONV)], axis=1)
    conv_s = _conv_sample(jnp.swapaxes(u_ext, 0, 1), dww, dwb, lng, lnb, w_pw)
    conv_s = jnp.swapaxes(conv_s, 0, 1).reshape(tsamp, d)
    q5 = q_s.reshape(db, ds, N_HEADS, 2, HEAD_DIM).transpose(0, 2, 3, 1, 4)
    zero = jnp.zeros_like(q5[:, :, 0])
    qbd = jnp.stack([jnp.concatenate([q5[:, :, 0], zero], axis=-1),
                     jnp.concatenate([zero, q5[:, :, 1]], axis=-1)], axis=2).reshape(db, N_HEADS * 2 * ds, V_DIM)
    pad_rows = NEW_PAD - ds * N_HEADS
    kn = jnp.pad(kb_s.reshape(db, ds * N_HEADS, V_DIM), ((0, 0), (0, pad_rows), (0, 0)))
    vn = jnp.pad(vb_s.reshape(db, ds * N_HEADS, V_DIM), ((0, 0), (0, pad_rows), (0, 0)))
    n_pool = cache_k.shape[1]
    ck = cache_k.reshape(cache_k.shape[0] * n_pool, PAGE_SIZE * N_HEADS, V_DIM)
    cv = cache_v.reshape(cache_v.shape[0] * n_pool, PAGE_SIZE * N_HEADS, V_DIM)
    o_s = _attn_sample(qbd, kn, vn, ck, cv, page_table + l * n_pool, lams, g_head).reshape(tsamp, vw)
    y_s = finish(xs_, o_s, conv_s, sgc_s, sga_s, 64, tsamp, 256).reshape(db, ds, d)

    k_prompt = k_p.reshape(1, bsz, seq, N_HEADS, 2 * HEAD_DIM)
    v_prompt = v_p.reshape(1, bsz, seq, N_HEADS, V_DIM)
    conv_prompt = u_p.reshape(bsz, seq, D_CONV)[:, seq - (CONV_W - 1):][None]
    k_sample = k_s.reshape(1, db, ds, N_HEADS, 2 * HEAD_DIM)
    v_sample = v_s.reshape(1, db, ds, N_HEADS, V_DIM)
    conv_sample = u_ext[:, ds:][None]
    return (y_p, y_s, k_prompt, v_prompt, conv_prompt, k_sample, v_sample, conv_sample)
```

```python
import functools
import math

import jax
import jax.numpy as jnp
from jax import lax
from jax.experimental import pallas as pl
from jax.experimental.pallas import tpu as pltpu

F32 = jnp.float32
BF16 = jnp.bfloat16
I32 = jnp.int32

D_MODEL = 2048
HEAD_DIM = 128
N_HEADS = D_MODEL // (2 * HEAD_DIM)
V_DIM = 2 * HEAD_DIM
D_CONV = D_MODEL // 2
CONV_W = 31
N_GROUPS = 4
EXPERTS_PER_GROUP = 8
N_EXPERTS = N_GROUPS * EXPERTS_PER_GROUP
D_EXPERT = D_MODEL // 4
PAGE_SIZE = 128
RMS_EPS = 1e-6
LN_EPS = 1e-5
ALIBI_MAX_EXP = 8.0
LAYER = 0
LAM_INIT = 0.8 - 0.6 * math.exp(-0.3 * LAYER)
LOG2E = math.log2(math.e)

LANES = 128
SUBLANES = 8
NEG_BIG = -1e30
DMA_UNROLL = 8
VMEM_LIMIT = 56 * 1024 * 1024


def _cparams(sem):
    return pltpu.CompilerParams(dimension_semantics=sem, vmem_limit_bytes=VMEM_LIMIT)


def _full(shape):
    return pl.BlockSpec(shape, lambda *_: (0,) * len(shape))


def _norm_matmul_kernel(x_ref, g_ref, *refs, n_w, n_out, epilogue):
    w_refs = refs[:n_w]
    o_refs = refs[n_w:n_w + n_out]
    h_ref = refs[n_w + n_out]

    @pl.when(pl.program_id(1) == 0)
    def _():
        x = x_ref[...]
        inv = lax.rsqrt(jnp.mean(x * x, axis=-1, keepdims=True) + RMS_EPS)
        h_ref[...] = (x * inv * g_ref[...]).astype(BF16)

    h = h_ref[...]
    zs = [jnp.dot(h, w[...], preferred_element_type=F32) for w in w_refs]
    for o_ref, val in zip(o_refs, epilogue(*zs)):
        o_ref[...] = val.astype(o_ref.dtype)


def _norm_matmul(x, g, ws, epilogue, out_dtypes, tm, tn, name):
    t, d = x.shape
    n = ws[0].shape[1]
    kern = functools.partial(_norm_matmul_kernel, n_w=len(ws), n_out=len(out_dtypes), epilogue=epilogue)
    return pl.pallas_call(
        kern,
        grid=(t // tm, n // tn),
        in_specs=[pl.BlockSpec((tm, d), lambda i, j: (i, 0)), _full((1, d))]
        + [pl.BlockSpec((d, tn), lambda i, j: (0, j)) for _ in ws],
        out_specs=[pl.BlockSpec((tm, tn), lambda i, j: (i, j)) for _ in out_dtypes],
        out_shape=[jax.ShapeDtypeStruct((t, n), dt) for dt in out_dtypes],
        scratch_shapes=[pltpu.VMEM((tm, d), BF16)],
        compiler_params=_cparams(("parallel", "arbitrary")),
        name=name,
    )(x, g, *ws)


def _glu_epilogue(a, b):
    return (a * jax.nn.sigmoid(b),)


def _qkv_epilogue(q, k, v):
    return (q * (LOG2E * HEAD_DIM ** -0.5), k, k, v, v)


def _gate_epilogue(gc, ga):
    return (jax.nn.sigmoid(gc), jax.nn.sigmoid(ga))


def _ln_swish_project(z, lng_ref, lnb_ref, wpw_ref):
    mu = jnp.mean(z, axis=-1, keepdims=True)
    zc = z - mu
    var = jnp.mean(zc * zc, axis=-1, keepdims=True)
    y = zc * lax.rsqrt(var + LN_EPS) * lng_ref[...] + lnb_ref[...]
    y = y * jax.nn.sigmoid(y)
    return jnp.dot(y.astype(BF16), wpw_ref[...], preferred_element_type=F32)


HALO = 32
CONV_ROWS = 128


def _conv_prompt_kernel(u_ref, halo_ref, dww_ref, dwb_ref, lng_ref, lnb_ref, wpw_ref, o_ref, ext_ref, z_ref, sh_ref,
                        *, ts):
    first = pl.program_id(1) == 0
    ext_ref[0:HALO, :] = jnp.where(first, 0.0, halo_ref[...])
    ext_ref[HALO:, :] = u_ref[...]
    base = HALO - (CONV_W - 1)
    for cb in range(D_CONV // LANES):
        cs = slice(cb * LANES, (cb + 1) * LANES)
        for shift in range(SUBLANES):
            span = ts + HALO - (SUBLANES if shift else 0)
            sh_ref[shift, 0:span, :] = ext_ref[shift:shift + span, cs]
        for r0 in range(0, ts, CONV_ROWS):
            acc = jnp.broadcast_to(dwb_ref[:, cs], (CONV_ROWS, LANES))
            for j in range(CONV_W):
                shift = (base + j) % SUBLANES
                a = base + j - shift + r0
                acc = acc + dww_ref[j:j + 1, cs] * sh_ref[shift, a:a + CONV_ROWS, :]
            z_ref[r0:r0 + CONV_ROWS, cs] = acc
    o_ref[...] = _ln_swish_project(z_ref[...], lng_ref, lnb_ref, wpw_ref)


def _conv_prompt(u, dww, dwb, lng, lnb, wpw, ts=256):
    b, s, c = u.shape
    d = wpw.shape[1]
    r = ts // HALO
    return pl.pallas_call(
        functools.partial(_conv_prompt_kernel, ts=ts),
        grid=(b, s // ts),
        in_specs=[
            pl.BlockSpec((None, ts, c), lambda bi, i: (bi, i, 0)),
            pl.BlockSpec((None, HALO, c), lambda bi, i: (bi, jnp.maximum(i * r - 1, 0), 0)),
            _full((CONV_W, c)), _full((1, c)), _full((1, c)), _full((1, c)), _full((c, d)),
        ],
        out_specs=pl.BlockSpec((None, ts, d), lambda bi, i: (bi, i, 0)),
        out_shape=jax.ShapeDtypeStruct((b, s, d), F32),
        scratch_shapes=[pltpu.VMEM((ts + HALO, c), F32), pltpu.VMEM((ts, c), F32),
                        pltpu.VMEM((SUBLANES, ts + HALO, LANES), F32)],
        compiler_params=_cparams(("parallel", "arbitrary")),
        name="conv_prompt",
    )(u, u, dww, dwb, lng, lnb, wpw)


def _conv_sample_kernel(ext_ref, dww_ref, dwb_ref, lng_ref, lnb_ref, wpw_ref, o_ref, *, n_new, bt):
    zs = []
    for t in range(n_new):
        acc = jnp.broadcast_to(dwb_ref[...], (bt, D_CONV))
        for j in range(CONV_W):
            acc = acc + dww_ref[j:j + 1, :] * ext_ref[t + j]
        zs.append(acc)
    out = _ln_swish_project(jnp.concatenate(zs, axis=0), lng_ref, lnb_ref, wpw_ref)
    for t in range(n_new):
        o_ref[t] = out[t * bt:(t + 1) * bt]


def _conv_sample(ext_t, dww, dwb, lng, lnb, wpw, bt=32):
    rows, b, c = ext_t.shape
    n_new = rows - (CONV_W - 1)
    d = wpw.shape[1]
    return pl.pallas_call(
        functools.partial(_conv_sample_kernel, n_new=n_new, bt=bt),
        grid=(b // bt,),
        in_specs=[
            pl.BlockSpec((rows, bt, c), lambda i: (0, i, 0)),
            _full((CONV_W, c)), _full((1, c)), _full((1, c)), _full((1, c)), _full((c, d)),
        ],
        out_specs=pl.BlockSpec((n_new, bt, d), lambda i: (0, i, 0)),
        out_shape=jax.ShapeDtypeStruct((n_new, b, d), F32),
        compiler_params=_cparams(("parallel",)),
        name="conv_sample",
    )(ext_t, dww, dwb, lng, lnb, wpw)


def _lambda_value(lq1_ref, lk1_ref, lq2_ref, lk2_ref):
    a = jnp.sum(lq1_ref[...] * lk1_ref[...], axis=-1, keepdims=True)
    b = jnp.sum(lq2_ref[...] * lk2_ref[...], axis=-1, keepdims=True)
    return jnp.exp(a) - jnp.exp(b) + LAM_INIT


def _head_norm(o, g_ref):
    inv = lax.rsqrt(jnp.mean(o * o, axis=-1, keepdims=True) + RMS_EPS)
    return o * inv * g_ref[...] * (1.0 - LAM_INIT)


def _alibi_slope(h, shape):
    return jnp.exp2(jnp.full(shape, -(ALIBI_MAX_EXP / N_HEADS), F32) * (h + 1).astype(F32))


def _nt_dot(a, b):
    return lax.dot_general(a, b, (((1,), (1,)), ((), ())), preferred_element_type=F32)


def _attn_prompt_kernel(q_ref, k_ref, v_ref, lq1_ref, lk1_ref, lq2_ref, lk2_ref, g_ref, o_ref, *, tq, seq):
    h = pl.program_id(1)
    kpos = lax.broadcasted_iota(I32, (1, seq), 1).astype(F32)
    bias = _alibi_slope(h, (1, seq)) * LOG2E * kpos
    lam = _lambda_value(lq1_ref, lk1_ref, lq2_ref, lk2_ref)
    visible = lax.broadcasted_iota(I32, (tq, tq), 1) <= lax.broadcasted_iota(I32, (tq, tq), 0)
    for i in range(seq // tq):
        lo, hi = i * tq, (i + 1) * tq
        outs = []
        for c in range(2):
            cs = slice(c * HEAD_DIM, (c + 1) * HEAD_DIM)
            q = q_ref[lo:hi, cs]
            s_d = jnp.where(visible, _nt_dot(q, k_ref[lo:hi, cs]) + bias[:, lo:hi], -jnp.inf)
            m = jnp.max(s_d, axis=-1, keepdims=True)
            if i > 0:
                s_o = _nt_dot(q, k_ref[0:lo, cs]) + bias[:, 0:lo]
                m = jnp.maximum(m, jnp.max(s_o, axis=-1, keepdims=True))
            p_d = jnp.exp2(s_d - m)
            l = jnp.sum(p_d, axis=-1, keepdims=True)
            pv = jnp.dot(p_d.astype(BF16), v_ref[lo:hi, :], preferred_element_type=F32)
            if i > 0:
                p_o = jnp.exp2(s_o - m)
                l = l + jnp.sum(p_o, axis=-1, keepdims=True)
                pv = pv + jnp.dot(p_o.astype(BF16), v_ref[0:lo, :], preferred_element_type=F32)
            outs.append(pv / l)
        o_ref[lo:hi, :] = _head_norm(outs[0] - lam * outs[1], g_ref).astype(o_ref.dtype)


def _attn_prompt(q, k, v, lams, g, tq=512):
    b, s, _ = q.shape
    w = 2 * HEAD_DIM
    spec = pl.BlockSpec((None, s, w), lambda bi, h: (bi, 0, h))
    return pl.pallas_call(
        functools.partial(_attn_prompt_kernel, tq=min(tq, s), seq=s),
        grid=(b, N_HEADS),
        in_specs=[spec, spec, spec] + [_full((1, HEAD_DIM))] * 4 + [_full((1, V_DIM))],
        out_specs=pl.BlockSpec((None, s, V_DIM), lambda bi, h: (bi, 0, h)),
        out_shape=jax.ShapeDtypeStruct((b, s, N_HEADS * V_DIM), BF16),
        compiler_params=_cparams(("parallel", "parallel")),
        name="attn_prompt",
    )(q, k, v, *lams, g)


NEW_PAD = 128


def _attn_sample_kernel(pt_ref, q_ref, kn_ref, vn_ref, bias_ref, off_ref, biasn_ref, *refs, pp, n_new, n_steps):
    k_refs = refs[:pp]
    v_refs = refs[pp:2 * pp]
    lq1_ref, lk1_ref, lq2_ref, lk2_ref, g_ref, o_ref, m_ref, l_ref, acc_ref = refs[2 * pp:]
    step = pl.program_id(1)
    q = q_ref[...]

    @pl.when(step == 0)
    def _():
        m_ref[...] = jnp.full(m_ref.shape, NEG_BIG, F32)
        l_ref[...] = jnp.zeros(l_ref.shape, F32)
        acc_ref[...] = jnp.zeros(acc_ref.shape, F32)

    def update(c, s, off, v):
        m_prev = m_ref[c][:, :1]
        m_new = jnp.maximum(m_prev, jnp.max(s, axis=-1, keepdims=True) + off)
        alpha = jnp.exp2(m_prev - m_new)
        p = jnp.exp2(s - (m_new - off))
        l_new = alpha * l_ref[c][:, :1] + jnp.sum(p, axis=-1, keepdims=True)
        acc_ref[c] = alpha * acc_ref[c] + jnp.dot(p.astype(BF16), v, preferred_element_type=F32)
        m_ref[c] = jnp.broadcast_to(m_new, m_ref.shape[1:])
        l_ref[c] = jnp.broadcast_to(l_new, l_ref.shape[1:])

    scores = [_nt_dot(q, k_refs[j][...].astype(BF16)) + bias_ref[...] for j in range(pp)]
    offs = [off_ref[:, :1] * (step * pp + j).astype(F32) for j in range(pp)]
    m_prevs = [m_ref[j][:, :1] for j in range(pp)]
    m_news = [jnp.maximum(m_prevs[j], jnp.max(scores[j], axis=-1, keepdims=True) + offs[j]) for j in range(pp)]
    ps = [jnp.exp2(scores[j] - (m_news[j] - offs[j])) for j in range(pp)]
    pvs = [jnp.dot(ps[j].astype(BF16), v_refs[j][...].astype(BF16), preferred_element_type=F32) for j in range(pp)]
    for j in range(pp):
        alpha = jnp.exp2(m_prevs[j] - m_news[j])
        l_new = alpha * l_ref[j][:, :1] + jnp.sum(ps[j], axis=-1, keepdims=True)
        acc_ref[j] = alpha * acc_ref[j] + pvs[j]
        m_ref[j] = jnp.broadcast_to(m_news[j], m_ref.shape[1:])
        l_ref[j] = jnp.broadcast_to(l_new, l_ref.shape[1:])

    @pl.when(step == n_steps - 1)
    def _():
        update(0, _nt_dot(q, kn_ref[...]) + biasn_ref[...], 0.0, vn_ref[...])
        m_all = m_ref[0][:, :1]
        for c in range(1, pp):
            m_all = jnp.maximum(m_all, m_ref[c][:, :1])
        l_all = jnp.zeros_like(m_all)
        acc_all = jnp.zeros(acc_ref.shape[1:], F32)
        for c in range(pp):
            w = jnp.exp2(m_ref[c][:, :1] - m_all)
            l_all = l_all + w * l_ref[c][:, :1]
            acc_all = acc_all + w * acc_ref[c]
        lam = _lambda_value(lq1_ref, lk1_ref, lq2_ref, lk2_ref)
        a = acc_all / l_all
        for h in range(N_HEADS):
            r0 = h * 2 * n_new
            o = a[r0:r0 + n_new] - lam * a[r0 + n_new:r0 + 2 * n_new]
            o_ref[:, h * V_DIM:(h + 1) * V_DIM] = _head_norm(o, g_ref).astype(o_ref.dtype)


def _attn_sample(q_all, kn, vn, cache_k, cache_v, page_table, lams, g, pp=8):
    db, n_pages = page_table.shape
    rows = q_all.shape[1]
    n_new = rows // (2 * N_HEADS)
    n_steps = n_pages // pp
    pr = PAGE_SIZE * N_HEADS

    slopes = 2.0 ** (-(ALIBI_MAX_EXP / N_HEADS) * (jnp.arange(rows, dtype=F32) // (2 * n_new) + 1.0))[:, None]
    row_head = (jnp.arange(rows, dtype=I32) // (2 * n_new))[:, None]
    row_q = (jnp.arange(rows, dtype=I32) % n_new)[:, None]
    col = jnp.arange(pr, dtype=I32)[None, :]
    bias = jnp.where(col % N_HEADS == row_head, slopes * LOG2E * (col // N_HEADS).astype(F32), -jnp.inf)
    off = jnp.broadcast_to(slopes * LOG2E * PAGE_SIZE, (rows, LANES))
    coln = jnp.arange(NEW_PAD, dtype=I32)[None, :]
    new_ok = (coln % N_HEADS == row_head) & (coln // N_HEADS <= row_q)
    biasn = jnp.where(new_ok, slopes * LOG2E * (n_pages * PAGE_SIZE + coln // N_HEADS).astype(F32), -jnp.inf)

    def page_spec(j):
        return pl.BlockSpec((None, pr, V_DIM), lambda b, s, pt: (pt[b * n_pages + s * pp + j], 0, 0))

    const = lambda shape: pl.BlockSpec(shape, lambda b, s, pt: (0,) * len(shape))
    per_seq = lambda r: pl.BlockSpec((None, r, V_DIM), lambda b, s, pt: (b, 0, 0))
    grid_spec = pltpu.PrefetchScalarGridSpec(
        num_scalar_prefetch=1,
        grid=(db, n_steps),
        in_specs=[per_seq(rows), per_seq(NEW_PAD), per_seq(NEW_PAD),
                  const((rows, pr)), const((rows, LANES)), const((rows, NEW_PAD))]
        + [page_spec(j) for j in range(pp)] + [page_spec(j) for j in range(pp)]
        + [const((1, HEAD_DIM))] * 4 + [const((1, V_DIM))],
        out_specs=pl.BlockSpec((None, n_new, N_HEADS * V_DIM), lambda b, s, pt: (b, 0, 0)),
        scratch_shapes=[pltpu.VMEM((pp, rows, LANES), F32), pltpu.VMEM((pp, rows, LANES), F32),
                        pltpu.VMEM((pp, rows, V_DIM), F32)],
    )
    return pl.pallas_call(
        functools.partial(_attn_sample_kernel, pp=pp, n_new=n_new, n_steps=n_steps),
        grid_spec=grid_spec,
        out_shape=jax.ShapeDtypeStruct((db, n_new, N_HEADS * V_DIM), BF16),
        compiler_params=_cparams(("parallel", "arbitrary")),
        name="attn_sample",
    )(page_table.reshape(-1), q_all, kn, vn, bias, off, biasn, *([cache_k] * pp), *([cache_v] * pp), *lams, g)


def _route(logits):
    lane = lax.broadcasted_iota(I32, logits.shape, 1)
    lane_f = lane.astype(F32)
    big = float(LANES)
    is_grp = lane < N_GROUPS
    lg = jnp.where(is_grp, logits, -jnp.inf)
    mg = jnp.max(lg, axis=-1, keepdims=True)
    g_sel = jnp.min(jnp.where(lg == mg, lane_f, big), axis=-1, keepdims=True)
    p_grp = 1.0 / jnp.sum(jnp.exp(lg - mg), axis=-1, keepdims=True)
    lo = N_GROUPS + EXPERTS_PER_GROUP * g_sel
    le = jnp.where((lane_f >= lo) & (lane_f < lo + EXPERTS_PER_GROUP), logits, -jnp.inf)
    m1 = jnp.max(le, axis=-1, keepdims=True)
    i1 = jnp.min(jnp.where(le == m1, lane_f, big), axis=-1, keepdims=True)
    le2 = jnp.where(lane_f == i1, -jnp.inf, le)
    m2 = jnp.max(le2, axis=-1, keepdims=True)
    i2 = jnp.min(jnp.where(le2 == m2, lane_f, big), axis=-1, keepdims=True)
    e2 = jnp.exp(m2 - m1)
    w1 = p_grp / (1.0 + e2)
    w2 = p_grp * e2 / (1.0 + e2)
    wts = jnp.where(lane == 0, w1, jnp.where(lane == 1, w2, 0.0))
    eid = jnp.where(lane == 0, i1 - N_GROUPS, jnp.where(lane == 1, i2 - N_GROUPS, 0.0)).astype(I32)
    return wts, eid


def _merge_kernel(x_ref, o_ref, conv_ref, sgc_ref, sga_ref, wa_ref, wo_ref, gf_ref, wr_ref, br_ref,
                  x1_ref, h2_ref, wts_ref, eid_ref):
    attn = jnp.dot(o_ref[...], wa_ref[...], preferred_element_type=F32)
    merged = sgc_ref[...].astype(F32) * conv_ref[...] + sga_ref[...].astype(F32) * attn
    x1 = x_ref[...] + jnp.dot(merged.astype(BF16), wo_ref[...], preferred_element_type=F32)
    x1_ref[...] = x1
    inv = lax.rsqrt(jnp.mean(x1 * x1, axis=-1, keepdims=True) + RMS_EPS)
    h2 = x1 * inv * gf_ref[...]
    h2_ref[...] = h2
    logits = jnp.dot(h2.astype(BF16), wr_ref[...], preferred_element_type=F32) + br_ref[...]
    wts, eid = _route(logits)
    wts_ref[...] = wts
    eid_ref[...] = eid


def _merge(x, o, conv, sgc, sga, wa, wo, gf, wr, br, tm=256):
    t, d = x.shape
    tm = min(tm, t)
    row = lambda w: pl.BlockSpec((tm, w), lambda i: (i, 0))
    single = lambda shape: pl.BlockSpec(shape, lambda i: (0,) * len(shape), pipeline_mode=pl.Buffered(1))
    return pl.pallas_call(
        _merge_kernel,
        grid=(t // tm,),
        in_specs=[row(d), row(d), row(d), row(d), row(d), single((d, d)), single((d, d)),
                  _full((1, d)), _full((d, LANES)), _full((1, LANES))],
        out_specs=[row(d), row(d), row(LANES), row(LANES)],
        out_shape=[jax.ShapeDtypeStruct((t, d), F32), jax.ShapeDtypeStruct((t, d), F32),
                   jax.ShapeDtypeStruct((t, LANES), F32), jax.ShapeDtypeStruct((t, LANES), I32)],
        compiler_params=_cparams(("parallel",)),
        name="merge_route",
    )(x, o, conv, sgc, sga, wa, wo, gf, wr, br)


def _rank_kernel(eid_ref, rank_ref, cnt_ref, carry_ref, *, tr):
    @pl.when(pl.program_id(0) == 0)
    def _():
        carry_ref[...] = jnp.zeros(carry_ref.shape, F32)

    eid = eid_ref[...]
    lane = lax.broadcasted_iota(I32, eid.shape, 1)
    e1 = jnp.sum(jnp.where(lane == 0, eid, 0), axis=-1, keepdims=True)
    e2 = jnp.sum(jnp.where(lane == 1, eid, 0), axis=-1, keepdims=True)
    hit1 = lane == e1
    hit2 = lane == e2
    onehot = jnp.where(hit1 | hit2, 1.0, 0.0)
    r = lax.broadcasted_iota(I32, (tr, tr), 0)
    c = lax.broadcasted_iota(I32, (tr, tr), 1)
    below = jnp.where(c < r, 1.0, 0.0).astype(BF16)
    carry = carry_ref[0:1, :]
    prefix = jnp.dot(below, onehot.astype(BF16), preferred_element_type=F32) + carry
    r1 = jnp.sum(jnp.where(hit1, prefix, 0.0), axis=-1, keepdims=True)
    r2 = jnp.sum(jnp.where(hit2, prefix, 0.0), axis=-1, keepdims=True)
    rank_ref[...] = jnp.where(lane == 0, r1, jnp.where(lane == 1, r2, 0.0)).astype(I32)
    total = carry + jnp.sum(onehot, axis=0, keepdims=True)
    carry_ref[...] = jnp.broadcast_to(total, carry_ref.shape)
    cnt_ref[...] = jnp.broadcast_to(total, cnt_ref.shape)


def _rank(eid, tr):
    t = eid.shape[0]
    return pl.pallas_call(
        functools.partial(_rank_kernel, tr=tr),
        grid=(t // tr,),
        in_specs=[pl.BlockSpec((tr, LANES), lambda i: (i, 0))],
        out_specs=[pl.BlockSpec((tr, LANES), lambda i: (i, 0)), _full((8, LANES))],
        out_shape=[jax.ShapeDtypeStruct((t, LANES), I32), jax.ShapeDtypeStruct((8, LANES), F32)],
        scratch_shapes=[pltpu.VMEM((8, LANES), F32)],
        compiler_params=_cparams(("arbitrary",)),
        name="expert_rank",
    )(eid)


def _dispatch_kernel(pos_ref, h_ref, xs_in_ref, xs_ref, sem, *, td):
    del xs_in_ref

    def row_copy(r, k):
        p = pos_ref[0, 0, 2 * r + k]
        return pltpu.make_async_copy(h_ref.at[pl.ds(r, 1)], xs_ref.at[pl.ds(p, 1)], sem)

    def start(r, carry):
        row_copy(r, 0).start()
        row_copy(r, 1).start(priority=1)
        return carry

    lax.fori_loop(0, td, start, 0, unroll=DMA_UNROLL)
    for _ in range(2):
        pltpu.make_async_copy(h_ref, xs_ref.at[pl.ds(0, td)], sem).wait()


def _dispatch(h2, pos, n_rows, td):
    t, d = h2.shape
    pos3 = pos.reshape(t // td, 1, 2 * td)
    xs0 = jnp.zeros((n_rows, d), F32)
    return pl.pallas_call(
        functools.partial(_dispatch_kernel, td=td),
        grid=(t // td,),
        in_specs=[pl.BlockSpec((1, 1, 2 * td), lambda i: (i, 0, 0), memory_space=pltpu.SMEM),
                  pl.BlockSpec((td, d), lambda i: (i, 0)),
                  pl.BlockSpec(memory_space=pl.ANY)],
        out_specs=pl.BlockSpec(memory_space=pl.ANY),
        out_shape=jax.ShapeDtypeStruct((n_rows, d), F32),
        scratch_shapes=[pltpu.SemaphoreType.DMA(())],
        input_output_aliases={2: 0},
        compiler_params=_cparams(("arbitrary",)),
        name="moe_dispatch",
    )(pos3, h2, xs0)


def _experts_kernel(te_ref, nu_ref, x_ref, wg_ref, wu_ref, wd_ref, y_ref, wgb_ref, wub_ref, wdb_ref):
    i = pl.program_id(0)
    active = i < nu_ref[0]
    new_expert = (i == 0) | (te_ref[i] != te_ref[jnp.maximum(i - 1, 0)])

    @pl.when(active & new_expert)
    def _():
        wgb_ref[...] = wg_ref[...].astype(BF16)
        wub_ref[...] = wu_ref[...].astype(BF16)
        wdb_ref[...] = wd_ref[...].astype(BF16)

    @pl.when(active)
    def _():
        x = x_ref[...].astype(BF16)
        hg = jnp.dot(x, wgb_ref[...], preferred_element_type=F32)
        hu = jnp.dot(x, wub_ref[...], preferred_element_type=F32)
        act = hg * jax.nn.sigmoid(hg) * hu
        y_ref[...] = jnp.dot(act.astype(BF16), wdb_ref[...], preferred_element_type=F32)

    @pl.when(pl.program_id(0) >= nu_ref[0])
    def _():
        y_ref[...] = jnp.zeros(y_ref.shape, F32)


def _experts(xs, tile_expert, n_used, wg, wu, wd, tme):
    n_rows, d = xs.shape
    f = wg.shape[2]
    n_tiles = n_rows // tme
    tile = lambda i, te, nu: (jnp.minimum(i, nu[0] - 1), 0)
    grid_spec = pltpu.PrefetchScalarGridSpec(
        num_scalar_prefetch=2,
        grid=(n_tiles,),
        in_specs=[pl.BlockSpec((tme, d), tile),
                  pl.BlockSpec((None, d, f), lambda i, te, nu: (te[i], 0, 0)),
                  pl.BlockSpec((None, d, f), lambda i, te, nu: (te[i], 0, 0)),
                  pl.BlockSpec((None, f, d), lambda i, te, nu: (te[i], 0, 0))],
        out_specs=pl.BlockSpec((tme, d), lambda i, te, nu: (i, 0)),
        scratch_shapes=[pltpu.VMEM((d, f), BF16), pltpu.VMEM((d, f), BF16), pltpu.VMEM((f, d), BF16)],
    )
    return pl.pallas_call(
        _experts_kernel,
        grid_spec=grid_spec,
        out_shape=jax.ShapeDtypeStruct((n_rows, d), F32),
        compiler_params=_cparams(("arbitrary",)),
        name="moe_experts",
    )(tile_expert, n_used, xs, wg, wu, wd)


def _combine_kernel(pos_ref, x1_ref, wts_ref, gfin_ref, y_ref, o_ref, buf_ref, sem, *, tc):
    def row_copy(r, k):
        p = pos_ref[0, 0, 2 * r + k]
        return pltpu.make_async_copy(y_ref.at[pl.ds(p, 1)], buf_ref.at[k, pl.ds(r, 1)], sem)

    def start(r, carry):
        row_copy(r, 0).start()
        row_copy(r, 1).start(priority=1)
        return carry

    lax.fori_loop(0, tc, start, 0, unroll=DMA_UNROLL)
    for k in range(2):
        pltpu.make_async_copy(y_ref.at[pl.ds(0, tc)], buf_ref.at[k], sem).wait()
    wts = wts_ref[...]
    x2 = x1_ref[...] + wts[:, 0:1] * buf_ref[0] + wts[:, 1:2] * buf_ref[1]
    inv = lax.rsqrt(jnp.mean(x2 * x2, axis=-1, keepdims=True) + RMS_EPS)
    o_ref[...] = x2 * inv * gfin_ref[...]


def _combine(x1, wts, pos, y, gfin, tc):
    t, d = x1.shape
    pos3 = pos.reshape(t // tc, 1, 2 * tc)
    return pl.pallas_call(
        functools.partial(_combine_kernel, tc=tc),
        grid=(t // tc,),
        in_specs=[pl.BlockSpec((1, 1, 2 * tc), lambda i: (i, 0, 0), memory_space=pltpu.SMEM),
                  pl.BlockSpec((tc, d), lambda i: (i, 0)),
                  pl.BlockSpec((tc, LANES), lambda i: (i, 0)),
                  _full((1, d)),
                  pl.BlockSpec(memory_space=pl.ANY)],
        out_specs=pl.BlockSpec((tc, d), lambda i: (i, 0)),
        out_shape=jax.ShapeDtypeStruct((t, d), F32),
        scratch_shapes=[pltpu.VMEM((2, tc, d), F32), pltpu.SemaphoreType.DMA(())],
        compiler_params=_cparams(("arbitrary",)),
        name="moe_combine",
    )(pos3, x1, wts, gfin, y)


def _moe(x1, h2, wts, eid, wg, wu, wd, gfin, tme, tr, td):
    t = x1.shape[0]
    td = min(td, t)
    rank, cnt = _rank(eid, tr)
    counts = cnt[0, :N_EXPERTS].astype(I32)
    padded = ((counts + tme - 1) // tme) * tme
    ends = jnp.cumsum(padded)
    offsets = ends - padded
    pos = (offsets[eid[:, :2]] + rank[:, :2]).astype(I32)
    n_tiles = (2 * t + N_EXPERTS * (tme - 1)) // tme
    tile_start = jnp.arange(n_tiles, dtype=I32) * tme
    tile_expert = jnp.minimum(
        jnp.sum((ends[None, :] <= tile_start[:, None]).astype(I32), axis=1), N_EXPERTS - 1).astype(I32)
    n_used = (ends[-1:] // tme).astype(I32)
    xs = _dispatch(h2, pos, n_tiles * tme, td)
    y = _experts(xs, tile_expert, n_used, wg, wu, wd, tme)
    return _combine(x1, wts, pos, y, gfin, td)


def kernel(x_prompt, x_sample, cache_k, cache_v, state_conv, page_table, norm_mix_g, w_in, conv_dw_w, conv_dw_b, conv_ln_g, conv_ln_b, conv_w_out, lambda_q1, lambda_k1, lambda_q2, lambda_k2, head_norm_g, attn_w_out, w_out, norm_ffn_g, router_group_w, router_group_b, router_expert_w, router_expert_b, expert_w_gate, expert_w_up, expert_w_down, final_norm_g):
    bsz, seq, d = x_prompt.shape
    db, ds, _ = x_sample.shape
    l = LAYER
    qkw = 2 * N_HEADS * HEAD_DIM
    vw = N_HEADS * V_DIM

    win = w_in[l]
    bounds = [0, D_CONV, 2 * D_CONV, 2 * D_CONV + qkw, 2 * D_CONV + 2 * qkw, 2 * D_CONV + 2 * qkw + vw,
              2 * D_CONV + 2 * qkw + vw + d, 2 * D_CONV + 2 * qkw + vw + 2 * d]
    w_a, w_b, w_q, w_k, w_v, w_gc, w_ga = [win[:, bounds[i]:bounds[i + 1]].astype(BF16) for i in range(7)]
    w_pw = conv_w_out[l].astype(BF16)
    w_ao = attn_w_out[l].astype(BF16)
    w_o = w_out[l].astype(BF16)
    w_r = jnp.zeros((d, LANES), F32).at[:, :N_GROUPS].set(router_group_w[l])
    w_r = w_r.at[:, N_GROUPS:N_GROUPS + N_EXPERTS].set(router_expert_w[l]).astype(BF16)
    b_r = jnp.zeros((1, LANES), F32).at[0, :N_GROUPS].set(router_group_b[l])
    b_r = b_r.at[0, N_GROUPS:N_GROUPS + N_EXPERTS].set(router_expert_b[l])
    wg, wu, wd = expert_w_gate[l], expert_w_up[l], expert_w_down[l]
    row = lambda a: a.reshape(1, -1)
    g_mix, g_ffn, g_fin = row(norm_mix_g[l]), row(norm_ffn_g[l]), row(final_norm_g)
    dww, dwb = conv_dw_w[l], row(conv_dw_b[l])
    lng, lnb = row(conv_ln_g[l]), row(conv_ln_b[l])
    lams = [row(lambda_q1[l]), row(lambda_k1[l]), row(lambda_q2[l]), row(lambda_k2[l])]
    g_head = row(head_norm_g[l])

    def project(x2d, tm):
        (u,) = _norm_matmul(x2d, g_mix, [w_a, w_b], _glu_epilogue, [F32], tm, 256, "proj_glu")
        q, k, kb, v, vb = _norm_matmul(x2d, g_mix, [w_q, w_k, w_v], _qkv_epilogue,
                                       [BF16, F32, BF16, F32, BF16], tm, 256, "proj_qkv")
        sgc, sga = _norm_matmul(x2d, g_mix, [w_gc, w_ga], _gate_epilogue, [BF16, BF16], tm, 256, "proj_gates")
        return u, q, k, kb, v, vb, sgc, sga

    def finish(x2d, o, conv, sgc, sga, tme, tr, td):
        x1, h2, wts, eid = _merge(x2d, o, conv, sgc, sga, w_ao, w_o, g_ffn, w_r, b_r)
        return _moe(x1, h2, wts, eid, wg, wu, wd, g_fin, tme, tr, td)

    tp = bsz * seq
    xp = x_prompt.reshape(tp, d)
    u_p, q_p, k_p, kb_p, v_p, vb_p, sgc_p, sga_p = project(xp, 1024)
    conv_p = _conv_prompt(u_p.reshape(bsz, seq, D_CONV), dww, dwb, lng, lnb, w_pw).reshape(tp, d)
    o_p = _attn_prompt(q_p.reshape(bsz, seq, qkw), kb_p.reshape(bsz, seq, qkw), vb_p.reshape(bsz, seq, vw),
                       lams, g_head).reshape(tp, vw)
    y_p = finish(xp, o_p, conv_p, sgc_p, sga_p, 256, 512, 256).reshape(bsz, seq, d)

    tsamp = db * ds
    xs_ = x_sample.reshape(tsamp, d)
    u_s, q_s, k_s, kb_s, v_s, vb_s, sgc_s, sga_s = project(xs_, tsamp)
    u_ext = jnp.concatenate([state_conv[l], u_s.reshape(db, ds, D_CONV)], axis=1)
    conv_s = _conv_sample(jnp.swapaxes(u_ext, 0, 1), dww, dwb, lng, lnb, w_pw)
    conv_s = jnp.swapaxes(conv_s, 0, 1).reshape(tsamp, d)
    q5 = q_s.reshape(db, ds, N_HEADS, 2, HEAD_DIM).transpose(0, 2, 3, 1, 4)
    zero = jnp.zeros_like(q5[:, :, 0])
    qbd = jnp.stack([jnp.concatenate([q5[:, :, 0], zero], axis=-1),
                     jnp.concatenate([zero, q5[:, :, 1]], axis=-1)], axis=2).reshape(db, N_HEADS * 2 * ds, V_DIM)
    pad_rows = NEW_PAD - ds * N_HEADS
    kn = jnp.pad(kb_s.reshape(db, ds * N_HEADS, V_DIM), ((0, 0), (0, pad_rows), (0, 0)))
    vn = jnp.pad(vb_s.reshape(db, ds * N_HEADS, V_DIM), ((0, 0), (0, pad_rows), (0, 0)))
    n_pool = cache_k.shape[1]
    ck = cache_k.reshape(cache_k.shape[0] * n_pool, PAGE_SIZE * N_HEADS, V_DIM)
    cv = cache_v.reshape(cache_v.shape[0] * n_pool, PAGE_SIZE * N_HEADS, V_DIM)
    o_s = _attn_sample(qbd, kn, vn, ck, cv, page_table + l * n_pool, lams, g_head).reshape(tsamp, vw)
    y_s = finish(xs_, o_s, conv_s, sgc_s, sga_s, 64, tsamp, 256).reshape(db, ds, d)

    k_prompt = k_p.reshape(1, bsz, seq, N_HEADS, 2 * HEAD_DIM)
    v_prompt = v_p.reshape(1, bsz, seq, N_HEADS, V_DIM)
    conv_prompt = u_p.reshape(bsz, seq, D_CONV)[:, seq - (CONV_W - 1):][None]
    k_sample = k_s.reshape(1, db, ds, N_HEADS, 2 * HEAD_DIM)
    v_sample = v_s.reshape(1, db, ds, N_HEADS, V_DIM)
    conv_sample = u_ext[:, ds:][None]
    return (y_p, y_s, k_prompt, v_prompt, conv_prompt, k_sample, v_sample, conv_sample)
```

```python
import functools
import math

import jax
import jax.numpy as jnp
from jax import lax
from jax.experimental import pallas as pl
from jax.experimental.pallas import tpu as pltpu

F32 = jnp.float32
BF16 = jnp.bfloat16
I32 = jnp.int32

D_MODEL = 2048
HEAD_DIM = 128
N_HEADS = D_MODEL // (2 * HEAD_DIM)
V_DIM = 2 * HEAD_DIM
D_CONV = D_MODEL // 2
CONV_W = 31
N_GROUPS = 4
EXPERTS_PER_GROUP = 8
N_EXPERTS = N_GROUPS * EXPERTS_PER_GROUP
D_EXPERT = D_MODEL // 4
PAGE_SIZE = 128
RMS_EPS = 1e-6
LN_EPS = 1e-5
ALIBI_MAX_EXP = 8.0
LAYER = 0
LAM_INIT = 0.8 - 0.6 * math.exp(-0.3 * LAYER)
LOG2E = math.log2(math.e)

LANES = 128
SUBLANES = 8
NEG_BIG = -1e30
PROMPT_TQ = 512
ATTN_PAGES = 8
MOE_TILE = 512
DMA_UNROLL = 8
VMEM_LIMIT = 56 * 1024 * 1024


def _cparams(sem):
    return pltpu.CompilerParams(dimension_semantics=sem, vmem_limit_bytes=VMEM_LIMIT)


def _full(shape):
    return pl.BlockSpec(shape, lambda *_: (0,) * len(shape))


def _norm_matmul_kernel(x_ref, g_ref, *refs, n_w, n_out, epilogue):
    w_refs = refs[:n_w]
    o_refs = refs[n_w:n_w + n_out]
    h_ref = refs[n_w + n_out]

    @pl.when(pl.program_id(1) == 0)
    def _():
        x = x_ref[...]
        inv = lax.rsqrt(jnp.mean(x * x, axis=-1, keepdims=True) + RMS_EPS)
        h_ref[...] = (x * inv * g_ref[...]).astype(BF16)

    h = h_ref[...]
    zs = [jnp.dot(h, w[...], preferred_element_type=F32) for w in w_refs]
    for o_ref, val in zip(o_refs, epilogue(*zs)):
        o_ref[...] = val.astype(o_ref.dtype)


def _norm_matmul(x, g, ws, epilogue, out_dtypes, tm, tn, name):
    t, d = x.shape
    n = ws[0].shape[1]
    kern = functools.partial(_norm_matmul_kernel, n_w=len(ws), n_out=len(out_dtypes), epilogue=epilogue)
    return pl.pallas_call(
        kern,
        grid=(t // tm, n // tn),
        in_specs=[pl.BlockSpec((tm, d), lambda i, j: (i, 0)), _full((1, d))]
        + [pl.BlockSpec((d, tn), lambda i, j: (0, j)) for _ in ws],
        out_specs=[pl.BlockSpec((tm, tn), lambda i, j: (i, j)) for _ in out_dtypes],
        out_shape=[jax.ShapeDtypeStruct((t, n), dt) for dt in out_dtypes],
        scratch_shapes=[pltpu.VMEM((tm, d), BF16)],
        compiler_params=_cparams(("parallel", "arbitrary")),
        name=name,
    )(x, g, *ws)


def _glu_epilogue(a, b):
    return (a * jax.nn.sigmoid(b),)


def _qkv_epilogue(q, k, v):
    return (q * (LOG2E * HEAD_DIM ** -0.5), k, k, v, v)


def _gate_epilogue(gc, ga):
    return (jax.nn.sigmoid(gc), jax.nn.sigmoid(ga))


def _ln_swish_project(z, lng_ref, lnb_ref, wpw_ref):
    mu = jnp.mean(z, axis=-1, keepdims=True)
    zc = z - mu
    var = jnp.mean(zc * zc, axis=-1, keepdims=True)
    y = zc * lax.rsqrt(var + LN_EPS) * lng_ref[...] + lnb_ref[...]
    y = y * jax.nn.sigmoid(y)
    return jnp.dot(y.astype(BF16), wpw_ref[...], preferred_element_type=F32)


HALO = 32
CONV_ROWS = 128


def _conv_prompt_kernel(u_ref, halo_ref, dww_ref, dwb_ref, lng_ref, lnb_ref, wpw_ref, o_ref, ext_ref, z_ref, sh_ref,
                        *, ts):
    first = pl.program_id(1) == 0
    ext_ref[0:HALO, :] = jnp.where(first, 0.0, halo_ref[...])
    ext_ref[HALO:, :] = u_ref[...]
    base = HALO - (CONV_W - 1)
    for cb in range(D_CONV // LANES):
        cs = slice(cb * LANES, (cb + 1) * LANES)
        for shift in range(SUBLANES):
            span = ts + HALO - (SUBLANES if shift else 0)
            sh_ref[shift, 0:span, :] = ext_ref[shift:shift + span, cs]
        for r0 in range(0, ts, CONV_ROWS):
            acc = jnp.broadcast_to(dwb_ref[:, cs], (CONV_ROWS, LANES))
            for j in range(CONV_W):
                shift = (base + j) % SUBLANES
                a = base + j - shift + r0
                acc = acc + dww_ref[j:j + 1, cs] * sh_ref[shift, a:a + CONV_ROWS, :]
            z_ref[r0:r0 + CONV_ROWS, cs] = acc
    o_ref[...] = _ln_swish_project(z_ref[...], lng_ref, lnb_ref, wpw_ref)


def _conv_prompt(u, dww, dwb, lng, lnb, wpw, ts=256):
    b, s, c = u.shape
    d = wpw.shape[1]
    r = ts // HALO
    return pl.pallas_call(
        functools.partial(_conv_prompt_kernel, ts=ts),
        grid=(b, s // ts),
        in_specs=[
            pl.BlockSpec((None, ts, c), lambda bi, i: (bi, i, 0)),
            pl.BlockSpec((None, HALO, c), lambda bi, i: (bi, jnp.maximum(i * r - 1, 0), 0)),
            _full((CONV_W, c)), _full((1, c)), _full((1, c)), _full((1, c)), _full((c, d)),
        ],
        out_specs=pl.BlockSpec((None, ts, d), lambda bi, i: (bi, i, 0)),
        out_shape=jax.ShapeDtypeStruct((b, s, d), F32),
        scratch_shapes=[pltpu.VMEM((ts + HALO, c), F32), pltpu.VMEM((ts, c), F32),
                        pltpu.VMEM((SUBLANES, ts + HALO, LANES), F32)],
        compiler_params=_cparams(("parallel", "arbitrary")),
        name="conv_prompt",
    )(u, u, dww, dwb, lng, lnb, wpw)


def _conv_sample_kernel(ext_ref, dww_ref, dwb_ref, lng_ref, lnb_ref, wpw_ref, o_ref, *, n_new, bt):
    zs = []
    for t in range(n_new):
        acc = jnp.broadcast_to(dwb_ref[...], (bt, D_CONV))
        for j in range(CONV_W):
            acc = acc + dww_ref[j:j + 1, :] * ext_ref[t + j]
        zs.append(acc)
    out = _ln_swish_project(jnp.concatenate(zs, axis=0), lng_ref, lnb_ref, wpw_ref)
    for t in range(n_new):
        o_ref[t] = out[t * bt:(t + 1) * bt]


def _conv_sample(ext_t, dww, dwb, lng, lnb, wpw, bt=32):
    rows, b, c = ext_t.shape
    n_new = rows - (CONV_W - 1)
    d = wpw.shape[1]
    return pl.pallas_call(
        functools.partial(_conv_sample_kernel, n_new=n_new, bt=bt),
        grid=(b // bt,),
        in_specs=[
            pl.BlockSpec((rows, bt, c), lambda i: (0, i, 0)),
            _full((CONV_W, c)), _full((1, c)), _full((1, c)), _full((1, c)), _full((c, d)),
        ],
        out_specs=pl.BlockSpec((n_new, bt, d), lambda i: (0, i, 0)),
        out_shape=jax.ShapeDtypeStruct((n_new, b, d), F32),
        compiler_params=_cparams(("parallel",)),
        name="conv_sample",
    )(ext_t, dww, dwb, lng, lnb, wpw)


def _lambda_value(lq1_ref, lk1_ref, lq2_ref, lk2_ref):
    a = jnp.sum(lq1_ref[...] * lk1_ref[...], axis=-1, keepdims=True)
    b = jnp.sum(lq2_ref[...] * lk2_ref[...], axis=-1, keepdims=True)
    return jnp.exp(a) - jnp.exp(b) + LAM_INIT


def _head_norm(o, g_ref):
    inv = lax.rsqrt(jnp.mean(o * o, axis=-1, keepdims=True) + RMS_EPS)
    return o * inv * g_ref[...] * (1.0 - LAM_INIT)


def _alibi_slope(h, shape):
    return jnp.exp2(jnp.full(shape, -(ALIBI_MAX_EXP / N_HEADS), F32) * (h + 1).astype(F32))


def _nt_dot(a, b):
    return lax.dot_general(a, b, (((1,), (1,)), ((), ())), preferred_element_type=F32)


def _prompt_query_tile(i, h, q_ref, k_ref, v_ref, lam_refs, g_ref, o_ref, *, tq):
    lo, hi = i * tq, (i + 1) * tq
    kpos = lax.broadcasted_iota(I32, (1, hi), 1).astype(F32)
    bias = _alibi_slope(h, (1, hi)) * LOG2E * kpos
    lam = _lambda_value(*lam_refs)
    visible = lax.broadcasted_iota(I32, (tq, tq), 1) <= lax.broadcasted_iota(I32, (tq, tq), 0)
    outs = []
    for c in range(2):
        cs = slice(c * HEAD_DIM, (c + 1) * HEAD_DIM)
        q = q_ref[lo:hi, cs]
        s_d = jnp.where(visible, _nt_dot(q, k_ref[lo:hi, cs]) + bias[:, lo:hi], -jnp.inf)
        m = jnp.max(s_d, axis=-1, keepdims=True)
        if i > 0:
            s_o = _nt_dot(q, k_ref[0:lo, cs]) + bias[:, 0:lo]
            m = jnp.maximum(m, jnp.max(s_o, axis=-1, keepdims=True))
        p_d = jnp.exp2(s_d - m)
        l = jnp.sum(p_d, axis=-1, keepdims=True)
        pv = jnp.dot(p_d.astype(BF16), v_ref[lo:hi, :], preferred_element_type=F32)
        if i > 0:
            p_o = jnp.exp2(s_o - m)
            l = l + jnp.sum(p_o, axis=-1, keepdims=True)
            pv = pv + jnp.dot(p_o.astype(BF16), v_ref[0:lo, :], preferred_element_type=F32)
        outs.append(pv / l)
    o_ref[lo:hi, :] = _head_norm(outs[0] - lam * outs[1], g_ref).astype(o_ref.dtype)


def _attn_prompt_kernel(q_ref, k_ref, v_ref, lq1_ref, lk1_ref, lq2_ref, lk2_ref, g_ref, o_ref, *, tq, seq):
    h = pl.program_id(1)
    for i in range(seq // tq):
        _prompt_query_tile(i, h, q_ref, k_ref, v_ref, (lq1_ref, lk1_ref, lq2_ref, lk2_ref), g_ref, o_ref, tq=tq)


def _attn_prompt(q, k, v, lams, g, tq=512):
    b, s, _ = q.shape
    w = 2 * HEAD_DIM
    spec = pl.BlockSpec((None, s, w), lambda bi, h: (bi, 0, h))
    return pl.pallas_call(
        functools.partial(_attn_prompt_kernel, tq=min(tq, s), seq=s),
        grid=(b, N_HEADS),
        in_specs=[spec, spec, spec] + [_full((1, HEAD_DIM))] * 4 + [_full((1, V_DIM))],
        out_specs=pl.BlockSpec((None, s, V_DIM), lambda bi, h: (bi, 0, h)),
        out_shape=jax.ShapeDtypeStruct((b, s, N_HEADS * V_DIM), BF16),
        compiler_params=_cparams(("parallel", "parallel")),
        name="attn_prompt",
    )(q, k, v, *lams, g)


NEW_PAD = 128


def _attn_sample_kernel(pt_ref, q_ref, kn_ref, vn_ref, bias_ref, off_ref, biasn_ref, *refs, pp, n_new, n_steps,
                        prompt_tiles):
    k_refs = refs[:pp]
    v_refs = refs[pp:2 * pp]
    lq1_ref, lk1_ref, lq2_ref, lk2_ref, g_ref = refs[2 * pp:2 * pp + 5]
    rest = refs[2 * pp + 5:]
    if prompt_tiles is not None:
        qp_ref, kp_ref, vp_ref, o_ref, op_ref, m_ref, l_ref, acc_ref = rest
        per_unit, tq, total = prompt_tiles
        flat = pl.program_id(0) * n_steps + pl.program_id(1)
        head = (flat // per_unit) % N_HEADS
        for i in range(per_unit):
            @pl.when((flat < total) & (flat % per_unit == i))
            def _(i=i):
                _prompt_query_tile(i, head, qp_ref, kp_ref, vp_ref, (lq1_ref, lk1_ref, lq2_ref, lk2_ref), g_ref,
                                   op_ref, tq=tq)
    else:
        o_ref, m_ref, l_ref, acc_ref = rest
    step = pl.program_id(1)
    q = q_ref[...]

    @pl.when(step == 0)
    def _():
        m_ref[...] = jnp.full(m_ref.shape, NEG_BIG, F32)
        l_ref[...] = jnp.zeros(l_ref.shape, F32)
        acc_ref[...] = jnp.zeros(acc_ref.shape, F32)

    def update(c, s, off, v):
        m_prev = m_ref[c][:, :1]
        m_new = jnp.maximum(m_prev, jnp.max(s, axis=-1, keepdims=True) + off)
        alpha = jnp.exp2(m_prev - m_new)
        p = jnp.exp2(s - (m_new - off))
        l_new = alpha * l_ref[c][:, :1] + jnp.sum(p, axis=-1, keepdims=True)
        acc_ref[c] = alpha * acc_ref[c] + jnp.dot(p.astype(BF16), v, preferred_element_type=F32)
        m_ref[c] = jnp.broadcast_to(m_new, m_ref.shape[1:])
        l_ref[c] = jnp.broadcast_to(l_new, l_ref.shape[1:])

    scores = [_nt_dot(q, k_refs[j][...].astype(BF16)) + bias_ref[...] for j in range(pp)]
    offs = [off_ref[:, :1] * (step * pp + j).astype(F32) for j in range(pp)]
    m_prevs = [m_ref[j][:, :1] for j in range(pp)]
    m_news = [jnp.maximum(m_prevs[j], jnp.max(scores[j], axis=-1, keepdims=True) + offs[j]) for j in range(pp)]
    ps = [jnp.exp2(scores[j] - (m_news[j] - offs[j])) for j in range(pp)]
    pvs = [jnp.dot(ps[j].astype(BF16), v_refs[j][...].astype(BF16), preferred_element_type=F32) for j in range(pp)]
    for j in range(pp):
        alpha = jnp.exp2(m_prevs[j] - m_news[j])
        l_new = alpha * l_ref[j][:, :1] + jnp.sum(ps[j], axis=-1, keepdims=True)
        acc_ref[j] = alpha * acc_ref[j] + pvs[j]
        m_ref[j] = jnp.broadcast_to(m_news[j], m_ref.shape[1:])
        l_ref[j] = jnp.broadcast_to(l_new, l_ref.shape[1:])

    @pl.when(step == n_steps - 1)
    def _():
        update(0, _nt_dot(q, kn_ref[...]) + biasn_ref[...], 0.0, vn_ref[...])
        m_all = m_ref[0][:, :1]
        for c in range(1, pp):
            m_all = jnp.maximum(m_all, m_ref[c][:, :1])
        l_all = jnp.zeros_like(m_all)
        acc_all = jnp.zeros(acc_ref.shape[1:], F32)
        for c in range(pp):
            w = jnp.exp2(m_ref[c][:, :1] - m_all)
            l_all = l_all + w * l_ref[c][:, :1]
            acc_all = acc_all + w * acc_ref[c]
        lam = _lambda_value(lq1_ref, lk1_ref, lq2_ref, lk2_ref)
        a = acc_all / l_all
        for h in range(N_HEADS):
            r0 = h * 2 * n_new
            o = a[r0:r0 + n_new] - lam * a[r0 + n_new:r0 + 2 * n_new]
            o_ref[:, h * V_DIM:(h + 1) * V_DIM] = _head_norm(o, g_ref).astype(o_ref.dtype)


def _attn_sample(q_all, kn, vn, cache_k, cache_v, page_table, lams, g, prompt=None, pp=8):
    db, n_pages = page_table.shape
    rows = q_all.shape[1]
    n_new = rows // (2 * N_HEADS)
    n_steps = n_pages // pp
    pr = PAGE_SIZE * N_HEADS

    slopes = 2.0 ** (-(ALIBI_MAX_EXP / N_HEADS) * (jnp.arange(rows, dtype=F32) // (2 * n_new) + 1.0))[:, None]
    row_head = (jnp.arange(rows, dtype=I32) // (2 * n_new))[:, None]
    row_q = (jnp.arange(rows, dtype=I32) % n_new)[:, None]
    col = jnp.arange(pr, dtype=I32)[None, :]
    bias = jnp.where(col % N_HEADS == row_head, slopes * LOG2E * (col // N_HEADS).astype(F32), -jnp.inf)
    off = jnp.broadcast_to(slopes * LOG2E * PAGE_SIZE, (rows, LANES))
    coln = jnp.arange(NEW_PAD, dtype=I32)[None, :]
    new_ok = (coln % N_HEADS == row_head) & (coln // N_HEADS <= row_q)
    biasn = jnp.where(new_ok, slopes * LOG2E * (n_pages * PAGE_SIZE + coln // N_HEADS).astype(F32), -jnp.inf)

    def page_spec(j):
        return pl.BlockSpec((None, pr, V_DIM), lambda b, s, pt: (pt[b * n_pages + s * pp + j], 0, 0))

    const = lambda shape: pl.BlockSpec(shape, lambda b, s, pt: (0,) * len(shape))
    per_seq = lambda r: pl.BlockSpec((None, r, V_DIM), lambda b, s, pt: (b, 0, 0))
    in_specs = ([per_seq(rows), per_seq(NEW_PAD), per_seq(NEW_PAD),
                 const((rows, pr)), const((rows, LANES)), const((rows, NEW_PAD))]
                + [page_spec(j) for j in range(pp)] + [page_spec(j) for j in range(pp)]
                + [const((1, HEAD_DIM))] * 4 + [const((1, V_DIM))])
    out_specs = [pl.BlockSpec((None, n_new, N_HEADS * V_DIM), lambda b, s, pt: (b, 0, 0))]
    out_shape = [jax.ShapeDtypeStruct((db, n_new, N_HEADS * V_DIM), BF16)]
    args = [q_all, kn, vn, bias, off, biasn, *([cache_k] * pp), *([cache_v] * pp), *lams, g]
    prompt_tiles = None
    if prompt is not None:
        qp, kp, vp, tq = prompt
        bsz, seq, _ = qp.shape
        per_unit = seq // tq
        units = bsz * N_HEADS
        prompt_tiles = (per_unit, tq, units * per_unit)
        assert db * n_steps >= units * per_unit, "not enough cache steps to carry the prompt tiles"

        def unit_map(b, s, pt):
            unit = jnp.minimum((b * n_steps + s) // per_unit, units - 1)
            return (unit // N_HEADS, 0, unit % N_HEADS)

        unit_spec = pl.BlockSpec((None, seq, V_DIM), unit_map)
        in_specs += [unit_spec] * 3
        out_specs.append(unit_spec)
        out_shape.append(jax.ShapeDtypeStruct((bsz, seq, N_HEADS * V_DIM), BF16))
        args += [qp, kp, vp]
    grid_spec = pltpu.PrefetchScalarGridSpec(
        num_scalar_prefetch=1,
        grid=(db, n_steps),
        in_specs=in_specs,
        out_specs=out_specs,
        scratch_shapes=[pltpu.VMEM((pp, rows, LANES), F32), pltpu.VMEM((pp, rows, LANES), F32),
                        pltpu.VMEM((pp, rows, V_DIM), F32)],
    )
    return pl.pallas_call(
        functools.partial(_attn_sample_kernel, pp=pp, n_new=n_new, n_steps=n_steps, prompt_tiles=prompt_tiles),
        grid_spec=grid_spec,
        out_shape=out_shape,
        compiler_params=_cparams(("arbitrary", "arbitrary")),
        name="attn_sample",
    )(page_table.reshape(-1), *args)


def _route(logits):
    lane = lax.broadcasted_iota(I32, logits.shape, 1)
    lane_f = lane.astype(F32)
    big = float(LANES)
    is_grp = lane < N_GROUPS
    lg = jnp.where(is_grp, logits, -jnp.inf)
    mg = jnp.max(lg, axis=-1, keepdims=True)
    g_sel = jnp.min(jnp.where(lg == mg, lane_f, big), axis=-1, keepdims=True)
    p_grp = 1.0 / jnp.sum(jnp.exp(lg - mg), axis=-1, keepdims=True)
    lo = N_GROUPS + EXPERTS_PER_GROUP * g_sel
    le = jnp.where((lane_f >= lo) & (lane_f < lo + EXPERTS_PER_GROUP), logits, -jnp.inf)
    m1 = jnp.max(le, axis=-1, keepdims=True)
    i1 = jnp.min(jnp.where(le == m1, lane_f, big), axis=-1, keepdims=True)
    le2 = jnp.where(lane_f == i1, -jnp.inf, le)
    m2 = jnp.max(le2, axis=-1, keepdims=True)
    i2 = jnp.min(jnp.where(le2 == m2, lane_f, big), axis=-1, keepdims=True)
    e2 = jnp.exp(m2 - m1)
    w1 = p_grp / (1.0 + e2)
    w2 = p_grp * e2 / (1.0 + e2)
    wts = jnp.where(lane == 0, w1, jnp.where(lane == 1, w2, 0.0))
    eid = jnp.where(lane == 0, i1 - N_GROUPS, jnp.where(lane == 1, i2 - N_GROUPS, 0.0)).astype(I32)
    return wts, eid


def _merge_kernel(x_ref, o_ref, conv_ref, sgc_ref, sga_ref, wa_ref, wo_ref, gf_ref, wr_ref, br_ref,
                  x1_ref, h2_ref, wts_ref, eid_ref):
    attn = jnp.dot(o_ref[...], wa_ref[...], preferred_element_type=F32)
    merged = sgc_ref[...].astype(F32) * conv_ref[...] + sga_ref[...].astype(F32) * attn
    x1 = x_ref[...] + jnp.dot(merged.astype(BF16), wo_ref[...], preferred_element_type=F32)
    x1_ref[...] = x1
    inv = lax.rsqrt(jnp.mean(x1 * x1, axis=-1, keepdims=True) + RMS_EPS)
    h2 = x1 * inv * gf_ref[...]
    h2_ref[...] = h2
    logits = jnp.dot(h2.astype(BF16), wr_ref[...], preferred_element_type=F32) + br_ref[...]
    wts, eid = _route(logits)
    wts_ref[...] = wts
    eid_ref[...] = eid


def _merge(x, o, conv, sgc, sga, wa, wo, gf, wr, br, tm=256):
    t, d = x.shape
    tm = min(tm, t)
    row = lambda w: pl.BlockSpec((tm, w), lambda i: (i, 0))
    single = lambda shape: pl.BlockSpec(shape, lambda i: (0,) * len(shape), pipeline_mode=pl.Buffered(1))
    return pl.pallas_call(
        _merge_kernel,
        grid=(t // tm,),
        in_specs=[row(d), row(d), row(d), row(d), row(d), single((d, d)), single((d, d)),
                  _full((1, d)), _full((d, LANES)), _full((1, LANES))],
        out_specs=[row(d), row(d), row(LANES), row(LANES)],
        out_shape=[jax.ShapeDtypeStruct((t, d), F32), jax.ShapeDtypeStruct((t, d), F32),
                   jax.ShapeDtypeStruct((t, LANES), F32), jax.ShapeDtypeStruct((t, LANES), I32)],
        compiler_params=_cparams(("parallel",)),
        name="merge_route",
    )(x, o, conv, sgc, sga, wa, wo, gf, wr, br)


def _rank_kernel(eid_ref, rank_ref, cnt_ref, carry_ref, *, tr):
    @pl.when(pl.program_id(0) == 0)
    def _():
        carry_ref[...] = jnp.zeros(carry_ref.shape, F32)

    eid = eid_ref[...]
    lane = lax.broadcasted_iota(I32, eid.shape, 1)
    e1 = jnp.sum(jnp.where(lane == 0, eid, 0), axis=-1, keepdims=True)
    e2 = jnp.sum(jnp.where(lane == 1, eid, 0), axis=-1, keepdims=True)
    hit1 = lane == e1
    hit2 = lane == e2
    onehot = jnp.where(hit1 | hit2, 1.0, 0.0)
    r = lax.broadcasted_iota(I32, (tr, tr), 0)
    c = lax.broadcasted_iota(I32, (tr, tr), 1)
    below = jnp.where(c < r, 1.0, 0.0).astype(BF16)
    carry = carry_ref[0:1, :]
    prefix = jnp.dot(below, onehot.astype(BF16), preferred_element_type=F32) + carry
    r1 = jnp.sum(jnp.where(hit1, prefix, 0.0), axis=-1, keepdims=True)
    r2 = jnp.sum(jnp.where(hit2, prefix, 0.0), axis=-1, keepdims=True)
    rank_ref[...] = jnp.where(lane == 0, r1, jnp.where(lane == 1, r2, 0.0)).astype(I32)
    total = carry + jnp.sum(onehot, axis=0, keepdims=True)
    carry_ref[...] = jnp.broadcast_to(total, carry_ref.shape)
    cnt_ref[...] = jnp.broadcast_to(total, cnt_ref.shape)


def _rank(eid, tr):
    t = eid.shape[0]
    return pl.pallas_call(
        functools.partial(_rank_kernel, tr=tr),
        grid=(t // tr,),
        in_specs=[pl.BlockSpec((tr, LANES), lambda i: (i, 0))],
        out_specs=[pl.BlockSpec((tr, LANES), lambda i: (i, 0)), _full((8, LANES))],
        out_shape=[jax.ShapeDtypeStruct((t, LANES), I32), jax.ShapeDtypeStruct((8, LANES), F32)],
        scratch_shapes=[pltpu.VMEM((8, LANES), F32)],
        compiler_params=_cparams(("arbitrary",)),
        name="expert_rank",
    )(eid)


def _dispatch_kernel(pos_ref, h_ref, xs_in_ref, xs_ref, sem, *, td):
    del xs_in_ref

    def row_copy(r, k):
        p = pos_ref[0, 0, 2 * r + k]
        return pltpu.make_async_copy(h_ref.at[pl.ds(r, 1)], xs_ref.at[pl.ds(p, 1)], sem)

    def start(r, carry):
        row_copy(r, 0).start()
        row_copy(r, 1).start(priority=1)
        return carry

    lax.fori_loop(0, td, start, 0, unroll=DMA_UNROLL)
    for _ in range(2):
        pltpu.make_async_copy(h_ref, xs_ref.at[pl.ds(0, td)], sem).wait()


def _dispatch(h2, pos, xs0, td):
    t, d = h2.shape
    n_rows = xs0.shape[0]
    pos3 = pos.reshape(t // td, 1, 2 * td)
    return pl.pallas_call(
        functools.partial(_dispatch_kernel, td=td),
        grid=(t // td,),
        in_specs=[pl.BlockSpec((1, 1, 2 * td), lambda i: (i, 0, 0), memory_space=pltpu.SMEM),
                  pl.BlockSpec((td, d), lambda i: (i, 0)),
                  pl.BlockSpec(memory_space=pl.ANY)],
        out_specs=pl.BlockSpec(memory_space=pl.ANY),
        out_shape=jax.ShapeDtypeStruct((n_rows, d), F32),
        scratch_shapes=[pltpu.SemaphoreType.DMA(())],
        input_output_aliases={2: 0},
        compiler_params=_cparams(("arbitrary",)),
        name="moe_dispatch",
    )(pos3, h2, xs0)


def _experts_kernel(te_ref, nu_ref, x_ref, wg_ref, wu_ref, wd_ref, y_ref, wgb_ref, wub_ref, wdb_ref):
    i = pl.program_id(0)
    active = i < nu_ref[0]
    new_expert = (i == 0) | (te_ref[i] != te_ref[jnp.maximum(i - 1, 0)])

    @pl.when(active & new_expert)
    def _():
        wgb_ref[...] = wg_ref[...].astype(BF16)
        wub_ref[...] = wu_ref[...].astype(BF16)
        wdb_ref[...] = wd_ref[...].astype(BF16)

    @pl.when(active)
    def _():
        x = x_ref[...].astype(BF16)
        hg = jnp.dot(x, wgb_ref[...], preferred_element_type=F32)
        hu = jnp.dot(x, wub_ref[...], preferred_element_type=F32)
        act = hg * jax.nn.sigmoid(hg) * hu
        y_ref[...] = jnp.dot(act.astype(BF16), wdb_ref[...], preferred_element_type=F32)

    @pl.when(pl.program_id(0) >= nu_ref[0])
    def _():
        y_ref[...] = jnp.zeros(y_ref.shape, F32)


def _experts(xs, tile_expert, n_used, wg, wu, wd, tme):
    n_rows, d = xs.shape
    f = wg.shape[2]
    n_tiles = n_rows // tme
    tile = lambda i, te, nu: (jnp.minimum(i, nu[0] - 1), 0)
    grid_spec = pltpu.PrefetchScalarGridSpec(
        num_scalar_prefetch=2,
        grid=(n_tiles,),
        in_specs=[pl.BlockSpec((tme, d), tile),
                  pl.BlockSpec((None, d, f), lambda i, te, nu: (te[i], 0, 0)),
                  pl.BlockSpec((None, d, f), lambda i, te, nu: (te[i], 0, 0)),
                  pl.BlockSpec((None, f, d), lambda i, te, nu: (te[i], 0, 0))],
        out_specs=pl.BlockSpec((tme, d), lambda i, te, nu: (i, 0)),
        scratch_shapes=[pltpu.VMEM((d, f), BF16), pltpu.VMEM((d, f), BF16), pltpu.VMEM((f, d), BF16)],
    )
    return pl.pallas_call(
        _experts_kernel,
        grid_spec=grid_spec,
        out_shape=jax.ShapeDtypeStruct((n_rows, d), F32),
        compiler_params=_cparams(("arbitrary",)),
        name="moe_experts",
    )(tile_expert, n_used, xs, wg, wu, wd)


def _combine_kernel(pos_ref, x1_ref, wts_ref, gfin_ref, y_ref, o_ref, buf_ref, sem, *, tc):
    def row_copy(r, k):
        p = pos_ref[0, 0, 2 * r + k]
        return pltpu.make_async_copy(y_ref.at[pl.ds(p, 1)], buf_ref.at[k, pl.ds(r, 1)], sem)

    def start(r, carry):
        row_copy(r, 0).start()
        row_copy(r, 1).start(priority=1)
        return carry

    lax.fori_loop(0, tc, start, 0, unroll=DMA_UNROLL)
    for k in range(2):
        pltpu.make_async_copy(y_ref.at[pl.ds(0, tc)], buf_ref.at[k], sem).wait()
    wts = wts_ref[...]
    x2 = x1_ref[...] + wts[:, 0:1] * buf_ref[0] + wts[:, 1:2] * buf_ref[1]
    inv = lax.rsqrt(jnp.mean(x2 * x2, axis=-1, keepdims=True) + RMS_EPS)
    o_ref[...] = x2 * inv * gfin_ref[...]


def _combine(x1, wts, pos, y, gfin, tc):
    t, d = x1.shape
    pos3 = pos.reshape(t // tc, 1, 2 * tc)
    return pl.pallas_call(
        functools.partial(_combine_kernel, tc=tc),
        grid=(t // tc,),
        in_specs=[pl.BlockSpec((1, 1, 2 * tc), lambda i: (i, 0, 0), memory_space=pltpu.SMEM),
                  pl.BlockSpec((tc, d), lambda i: (i, 0)),
                  pl.BlockSpec((tc, LANES), lambda i: (i, 0)),
                  _full((1, d)),
                  pl.BlockSpec(memory_space=pl.ANY)],
        out_specs=pl.BlockSpec((tc, d), lambda i: (i, 0)),
        out_shape=jax.ShapeDtypeStruct((t, d), F32),
        scratch_shapes=[pltpu.VMEM((2, tc, d), F32), pltpu.SemaphoreType.DMA(())],
        compiler_params=_cparams(("arbitrary",)),
        name="moe_combine",
    )(pos3, x1, wts, gfin, y)


def _moe(groups, wg, wu, wd, gfin, tme, tr, td):
    sizes = [g[0].shape[0] for g in groups]
    t = sum(sizes)
    eid = jnp.concatenate([g[3] for g in groups], axis=0)
    rank, cnt = _rank(eid, min(tr, t))
    counts = cnt[0, :N_EXPERTS].astype(I32)
    padded = ((counts + tme - 1) // tme) * tme
    ends = jnp.cumsum(padded)
    offsets = ends - padded
    pos = (offsets[eid[:, :2]] + rank[:, :2]).astype(I32)
    n_tiles = (2 * t + N_EXPERTS * (tme - 1)) // tme
    tile_start = jnp.arange(n_tiles, dtype=I32) * tme
    tile_expert = jnp.minimum(
        jnp.sum((ends[None, :] <= tile_start[:, None]).astype(I32), axis=1), N_EXPERTS - 1).astype(I32)
    n_used = (ends[-1:] // tme).astype(I32)
    starts = [sum(sizes[:i]) for i in range(len(sizes))]
    xs = jnp.zeros((n_tiles * tme, groups[0][1].shape[1]), F32)
    for (x1, h2, wts, _), s0, n in zip(groups, starts, sizes):
        xs = _dispatch(h2, pos[s0:s0 + n], xs, min(td, n))
    y = _experts(xs, tile_expert, n_used, wg, wu, wd, tme)
    return [_combine(x1, wts, pos[s0:s0 + n], y, gfin, min(td, n))
            for (x1, h2, wts, _), s0, n in zip(groups, starts, sizes)]


def kernel(x_prompt, x_sample, cache_k, cache_v, state_conv, page_table, norm_mix_g, w_in, conv_dw_w, conv_dw_b, conv_ln_g, conv_ln_b, conv_w_out, lambda_q1, lambda_k1, lambda_q2, lambda_k2, head_norm_g, attn_w_out, w_out, norm_ffn_g, router_group_w, router_group_b, router_expert_w, router_expert_b, expert_w_gate, expert_w_up, expert_w_down, final_norm_g):
    bsz, seq, d = x_prompt.shape
    db, ds, _ = x_sample.shape
    l = LAYER
    qkw = 2 * N_HEADS * HEAD_DIM
    vw = N_HEADS * V_DIM

    win = w_in[l]
    bounds = [0, D_CONV, 2 * D_CONV, 2 * D_CONV + qkw, 2 * D_CONV + 2 * qkw, 2 * D_CONV + 2 * qkw + vw,
              2 * D_CONV + 2 * qkw + vw + d, 2 * D_CONV + 2 * qkw + vw + 2 * d]
    w_a, w_b, w_q, w_k, w_v, w_gc, w_ga = [win[:, bounds[i]:bounds[i + 1]].astype(BF16) for i in range(7)]
    w_pw = conv_w_out[l].astype(BF16)
    w_ao = attn_w_out[l].astype(BF16)
    w_o = w_out[l].astype(BF16)
    w_r = jnp.zeros((d, LANES), F32).at[:, :N_GROUPS].set(router_group_w[l])
    w_r = w_r.at[:, N_GROUPS:N_GROUPS + N_EXPERTS].set(router_expert_w[l]).astype(BF16)
    b_r = jnp.zeros((1, LANES), F32).at[0, :N_GROUPS].set(router_group_b[l])
    b_r = b_r.at[0, N_GROUPS:N_GROUPS + N_EXPERTS].set(router_expert_b[l])
    wg, wu, wd = expert_w_gate[l], expert_w_up[l], expert_w_down[l]
    row = lambda a: a.reshape(1, -1)
    g_mix, g_ffn, g_fin = row(norm_mix_g[l]), row(norm_ffn_g[l]), row(final_norm_g)
    dww, dwb = conv_dw_w[l], row(conv_dw_b[l])
    lng, lnb = row(conv_ln_g[l]), row(conv_ln_b[l])
    lams = [row(lambda_q1[l]), row(lambda_k1[l]), row(lambda_q2[l]), row(lambda_k2[l])]
    g_head = row(head_norm_g[l])

    def project(x2d, tm):
        (u,) = _norm_matmul(x2d, g_mix, [w_a, w_b], _glu_epilogue, [F32], tm, 256, "proj_glu")
        q, k, kb, v, vb = _norm_matmul(x2d, g_mix, [w_q, w_k, w_v], _qkv_epilogue,
                                       [BF16, F32, BF16, F32, BF16], tm, 256, "proj_qkv")
        sgc, sga = _norm_matmul(x2d, g_mix, [w_gc, w_ga], _gate_epilogue, [BF16, BF16], tm, 256, "proj_gates")
        return u, q, k, kb, v, vb, sgc, sga

    def merge(x2d, o, conv, sgc, sga):
        return _merge(x2d, o, conv, sgc, sga, w_ao, w_o, g_ffn, w_r, b_r)

    tp = bsz * seq
    xp = x_prompt.reshape(tp, d)
    u_p, q_p, k_p, kb_p, v_p, vb_p, sgc_p, sga_p = project(xp, 1024)
    conv_p = _conv_prompt(u_p.reshape(bsz, seq, D_CONV), dww, dwb, lng, lnb, w_pw).reshape(tp, d)
    qkv_p = (q_p.reshape(bsz, seq, qkw), kb_p.reshape(bsz, seq, qkw), vb_p.reshape(bsz, seq, vw))

    tsamp = db * ds
    xs_ = x_sample.reshape(tsamp, d)
    u_s, q_s, k_s, kb_s, v_s, vb_s, sgc_s, sga_s = project(xs_, tsamp)
    u_ext = jnp.concatenate([state_conv[l], u_s.reshape(db, ds, D_CONV)], axis=1)
    conv_s = _conv_sample(jnp.swapaxes(u_ext, 0, 1), dww, dwb, lng, lnb, w_pw)
    conv_s = jnp.swapaxes(conv_s, 0, 1).reshape(tsamp, d)
    q5 = q_s.reshape(db, ds, N_HEADS, 2, HEAD_DIM).transpose(0, 2, 3, 1, 4)
    zero = jnp.zeros_like(q5[:, :, 0])
    qbd = jnp.stack([jnp.concatenate([q5[:, :, 0], zero], axis=-1),
                     jnp.concatenate([zero, q5[:, :, 1]], axis=-1)], axis=2).reshape(db, N_HEADS * 2 * ds, V_DIM)
    pad_rows = NEW_PAD - ds * N_HEADS
    kn = jnp.pad(kb_s.reshape(db, ds * N_HEADS, V_DIM), ((0, 0), (0, pad_rows), (0, 0)))
    vn = jnp.pad(vb_s.reshape(db, ds * N_HEADS, V_DIM), ((0, 0), (0, pad_rows), (0, 0)))
    n_pool = cache_k.shape[1]
    ck = cache_k.reshape(cache_k.shape[0] * n_pool, PAGE_SIZE * N_HEADS, V_DIM)
    cv = cache_v.reshape(cache_v.shape[0] * n_pool, PAGE_SIZE * N_HEADS, V_DIM)
    tq = min(PROMPT_TQ, seq)
    pt = page_table + l * n_pool
    if db * (page_table.shape[1] // ATTN_PAGES) >= bsz * N_HEADS * (seq // tq):
        o_s, o_p = _attn_sample(qbd, kn, vn, ck, cv, pt, lams, g_head, prompt=(*qkv_p, tq), pp=ATTN_PAGES)
    else:
        (o_s,) = _attn_sample(qbd, kn, vn, ck, cv, pt, lams, g_head, pp=ATTN_PAGES)
        o_p = _attn_prompt(*qkv_p, lams, g_head, tq=tq)
    group_p = merge(xp, o_p.reshape(tp, vw), conv_p, sgc_p, sga_p)
    group_s = merge(xs_, o_s.reshape(tsamp, vw), conv_s, sgc_s, sga_s)

    t_all = tp + tsamp
    tr = next(c for c in (512, 256, 128) if t_all % c == 0)
    y_p, y_s = _moe([group_p, group_s], wg, wu, wd, g_fin, MOE_TILE, tr, 256)
    y_p = y_p.reshape(bsz, seq, d)
    y_s = y_s.reshape(db, ds, d)

    k_prompt = k_p.reshape(1, bsz, seq, N_HEADS, 2 * HEAD_DIM)
    v_prompt = v_p.reshape(1, bsz, seq, N_HEADS, V_DIM)
    conv_prompt = u_p.reshape(bsz, seq, D_CONV)[:, seq - (CONV_W - 1):][None]
    k_sample = k_s.reshape(1, db, ds, N_HEADS, 2 * HEAD_DIM)
    v_sample = v_s.reshape(1, db, ds, N_HEADS, V_DIM)
    conv_sample = u_ext[:, ds:][None]
    return (y_p, y_s, k_prompt, v_prompt, conv_prompt, k_sample, v_sample, conv_sample)
```

```python
import functools
import math

import jax
import jax.numpy as jnp
from jax import lax
from jax.experimental import pallas as pl
from jax.experimental.pallas import tpu as pltpu

F32 = jnp.float32
BF16 = jnp.bfloat16
I32 = jnp.int32

D_MODEL = 2048
HEAD_DIM = 128
N_HEADS = D_MODEL // (2 * HEAD_DIM)
V_DIM = 2 * HEAD_DIM
D_CONV = D_MODEL // 2
CONV_W = 31
N_GROUPS = 4
EXPERTS_PER_GROUP = 8
N_EXPERTS = N_GROUPS * EXPERTS_PER_GROUP
D_EXPERT = D_MODEL // 4
PAGE_SIZE = 128
RMS_EPS = 1e-6
LN_EPS = 1e-5
ALIBI_MAX_EXP = 8.0
LAYER = 0
LAM_INIT = 0.8 - 0.6 * math.exp(-0.3 * LAYER)
LOG2E = math.log2(math.e)

LANES = 128
SUBLANES = 8
NEG_BIG = -1e30
PROMPT_TQ = 512
ATTN_PAGES = 8
MOE_TILE = 512
DMA_UNROLL = 8
VMEM_LIMIT = 56 * 1024 * 1024


def _cparams(sem):
    return pltpu.CompilerParams(dimension_semantics=sem, vmem_limit_bytes=VMEM_LIMIT)


def _full(shape):
    return pl.BlockSpec(shape, lambda *_: (0,) * len(shape))


PROJ_SEGMENTS = (("glu_b", D_CONV), ("glu_a", D_CONV), ("q", D_MODEL), ("k", D_MODEL), ("v", D_MODEL),
                 ("g_conv", D_MODEL), ("g_attn", D_MODEL))


def _proj_kernel(x_ref, g_ref, w_ref, u_ref, q_ref, k_ref, kb_ref, v_ref, vb_ref, gc_ref, ga_ref, h_ref, sig_ref,
                 *, tn):
    j = pl.program_id(1)

    @pl.when(j == 0)
    def _():
        x = x_ref[...]
        inv = lax.rsqrt(jnp.mean(x * x, axis=-1, keepdims=True) + RMS_EPS)
        h_ref[...] = (x * inv * g_ref[...]).astype(BF16)

    z = jnp.dot(h_ref[...], w_ref[...], preferred_element_type=F32)

    def glu_b(jj):
        sig_ref[jj] = jax.nn.sigmoid(z)

    def glu_a(jj):
        u_ref[...] = z * sig_ref[jj]

    def q_seg(jj):
        q_ref[...] = (z * (LOG2E * HEAD_DIM ** -0.5)).astype(q_ref.dtype)

    def k_seg(jj):
        k_ref[...] = z
        kb_ref[...] = z.astype(kb_ref.dtype)

    def v_seg(jj):
        v_ref[...] = z
        vb_ref[...] = z.astype(vb_ref.dtype)

    def gc_seg(jj):
        gc_ref[...] = jax.nn.sigmoid(z).astype(gc_ref.dtype)

    def ga_seg(jj):
        ga_ref[...] = jax.nn.sigmoid(z).astype(ga_ref.dtype)

    start = 0
    for (_, width), fn in zip(PROJ_SEGMENTS, (glu_b, glu_a, q_seg, k_seg, v_seg, gc_seg, ga_seg)):
        blocks = width // tn
        if fn in (glu_b, glu_a):
            for jj in range(blocks):
                pl.when(j == start + jj)(functools.partial(fn, jj))
        else:
            pl.when((j >= start) & (j < start + blocks))(functools.partial(fn, 0))
        start += blocks


def _project(x, g, w_all, tm, tn=512):
    t, d = x.shape
    n = w_all.shape[1]
    starts, s0 = {}, 0
    for name, width in PROJ_SEGMENTS:
        starts[name] = (s0 // tn, width // tn)
        s0 += width

    def seg_spec(name):
        first, blocks = starts[name]
        return pl.BlockSpec((tm, tn), lambda i, j: (i, jnp.clip(j - first, 0, blocks - 1)))

    outs = [("glu_a", D_CONV, F32), ("q", D_MODEL, BF16), ("k", D_MODEL, F32), ("k", D_MODEL, BF16),
            ("v", D_MODEL, F32), ("v", D_MODEL, BF16), ("g_conv", D_MODEL, BF16), ("g_attn", D_MODEL, BF16)]
    return pl.pallas_call(
        functools.partial(_proj_kernel, tn=tn),
        grid=(t // tm, n // tn),
        in_specs=[pl.BlockSpec((tm, d), lambda i, j: (i, 0)), _full((1, d)),
                  pl.BlockSpec((d, tn), lambda i, j: (0, j))],
        out_specs=[seg_spec(name) for name, _, _ in outs],
        out_shape=[jax.ShapeDtypeStruct((t, width), dt) for _, width, dt in outs],
        scratch_shapes=[pltpu.VMEM((tm, d), BF16), pltpu.VMEM((D_CONV // tn, tm, tn), F32)],
        compiler_params=_cparams(("arbitrary", "arbitrary")),
        name="proj",
    )(x, g, w_all)


def _ln_swish_project(z, lng_ref, lnb_ref, wpw_ref):
    mu = jnp.mean(z, axis=-1, keepdims=True)
    zc = z - mu
    var = jnp.mean(zc * zc, axis=-1, keepdims=True)
    y = zc * lax.rsqrt(var + LN_EPS) * lng_ref[...] + lnb_ref[...]
    y = y * jax.nn.sigmoid(y)
    return jnp.dot(y.astype(BF16), wpw_ref[...], preferred_element_type=F32)


HALO = 32
CONV_ROWS = 128


def _conv_prompt_kernel(u_ref, halo_ref, dww_ref, dwb_ref, lng_ref, lnb_ref, wpw_ref, o_ref, ext_ref, z_ref, sh_ref,
                        *, ts):
    first = pl.program_id(1) == 0
    ext_ref[0:HALO, :] = jnp.where(first, 0.0, halo_ref[...])
    ext_ref[HALO:, :] = u_ref[...]
    base = HALO - (CONV_W - 1)
    for cb in range(D_CONV // LANES):
        cs = slice(cb * LANES, (cb + 1) * LANES)
        for shift in range(SUBLANES):
            span = ts + HALO - (SUBLANES if shift else 0)
            sh_ref[shift, 0:span, :] = ext_ref[shift:shift + span, cs]
        for r0 in range(0, ts, CONV_ROWS):
            acc = jnp.broadcast_to(dwb_ref[:, cs], (CONV_ROWS, LANES))
            for j in range(CONV_W):
                shift = (base + j) % SUBLANES
                a = base + j - shift + r0
                acc = acc + dww_ref[j:j + 1, cs] * sh_ref[shift, a:a + CONV_ROWS, :]
            z_ref[r0:r0 + CONV_ROWS, cs] = acc
    o_ref[...] = _ln_swish_project(z_ref[...], lng_ref, lnb_ref, wpw_ref)


def _conv_prompt(u, dww, dwb, lng, lnb, wpw, ts=256):
    b, s, c = u.shape
    d = wpw.shape[1]
    r = ts // HALO
    return pl.pallas_call(
        functools.partial(_conv_prompt_kernel, ts=ts),
        grid=(b, s // ts),
        in_specs=[
            pl.BlockSpec((None, ts, c), lambda bi, i: (bi, i, 0)),
            pl.BlockSpec((None, HALO, c), lambda bi, i: (bi, jnp.maximum(i * r - 1, 0), 0)),
            _full((CONV_W, c)), _full((1, c)), _full((1, c)), _full((1, c)), _full((c, d)),
        ],
        out_specs=pl.BlockSpec((None, ts, d), lambda bi, i: (bi, i, 0)),
        out_shape=jax.ShapeDtypeStruct((b, s, d), F32),
        scratch_shapes=[pltpu.VMEM((ts + HALO, c), F32), pltpu.VMEM((ts, c), F32),
                        pltpu.VMEM((SUBLANES, ts + HALO, LANES), F32)],
        compiler_params=_cparams(("parallel", "arbitrary")),
        name="conv_prompt",
    )(u, u, dww, dwb, lng, lnb, wpw)


def _conv_sample_kernel(ext_ref, dww_ref, dwb_ref, lng_ref, lnb_ref, wpw_ref, o_ref, *, n_new, bt):
    zs = []
    for t in range(n_new):
        acc = jnp.broadcast_to(dwb_ref[...], (bt, D_CONV))
        for j in range(CONV_W):
            acc = acc + dww_ref[j:j + 1, :] * ext_ref[t + j]
        zs.append(acc)
    out = _ln_swish_project(jnp.concatenate(zs, axis=0), lng_ref, lnb_ref, wpw_ref)
    for t in range(n_new):
        o_ref[t] = out[t * bt:(t + 1) * bt]


def _conv_sample(ext_t, dww, dwb, lng, lnb, wpw, bt=32):
    rows, b, c = ext_t.shape
    n_new = rows - (CONV_W - 1)
    d = wpw.shape[1]
    return pl.pallas_call(
        functools.partial(_conv_sample_kernel, n_new=n_new, bt=bt),
        grid=(b // bt,),
        in_specs=[
            pl.BlockSpec((rows, bt, c), lambda i: (0, i, 0)),
            _full((CONV_W, c)), _full((1, c)), _full((1, c)), _full((1, c)), _full((c, d)),
        ],
        out_specs=pl.BlockSpec((n_new, bt, d), lambda i: (0, i, 0)),
        out_shape=jax.ShapeDtypeStruct((n_new, b, d), F32),
        compiler_params=_cparams(("parallel",)),
        name="conv_sample",
    )(ext_t, dww, dwb, lng, lnb, wpw)


def _lambda_value(lq1_ref, lk1_ref, lq2_ref, lk2_ref):
    a = jnp.sum(lq1_ref[...] * lk1_ref[...], axis=-1, keepdims=True)
    b = jnp.sum(lq2_ref[...] * lk2_ref[...], axis=-1, keepdims=True)
    return jnp.exp(a) - jnp.exp(b) + LAM_INIT


def _head_norm(o, g_ref):
    inv = lax.rsqrt(jnp.mean(o * o, axis=-1, keepdims=True) + RMS_EPS)
    return o * inv * g_ref[...] * (1.0 - LAM_INIT)


def _alibi_slope(h, shape):
    return jnp.exp2(jnp.full(shape, -(ALIBI_MAX_EXP / N_HEADS), F32) * (h + 1).astype(F32))


def _nt_dot(a, b):
    return lax.dot_general(a, b, (((1,), (1,)), ((), ())), preferred_element_type=F32)


def _prompt_query_tile(i, h, q_ref, k_ref, v_ref, lam_refs, g_ref, o_ref, *, tq):
    lo, hi = i * tq, (i + 1) * tq
    kpos = lax.broadcasted_iota(I32, (1, hi), 1).astype(F32)
    bias = _alibi_slope(h, (1, hi)) * LOG2E * kpos
    lam = _lambda_value(*lam_refs)
    visible = lax.broadcasted_iota(I32, (tq, tq), 1) <= lax.broadcasted_iota(I32, (tq, tq), 0)
    outs = []
    for c in range(2):
        cs = slice(c * HEAD_DIM, (c + 1) * HEAD_DIM)
        q = q_ref[lo:hi, cs]
        s_d = jnp.where(visible, _nt_dot(q, k_ref[lo:hi, cs]) + bias[:, lo:hi], -jnp.inf)
        m = jnp.max(s_d, axis=-1, keepdims=True)
        if i > 0:
            s_o = _nt_dot(q, k_ref[0:lo, cs]) + bias[:, 0:lo]
            m = jnp.maximum(m, jnp.max(s_o, axis=-1, keepdims=True))
        p_d = jnp.exp2(s_d - m)
        l = jnp.sum(p_d, axis=-1, keepdims=True)
        pv = jnp.dot(p_d.astype(BF16), v_ref[lo:hi, :], preferred_element_type=F32)
        if i > 0:
            p_o = jnp.exp2(s_o - m)
            l = l + jnp.sum(p_o, axis=-1, keepdims=True)
            pv = pv + jnp.dot(p_o.astype(BF16), v_ref[0:lo, :], preferred_element_type=F32)
        outs.append(pv / l)
    o_ref[lo:hi, :] = _head_norm(outs[0] - lam * outs[1], g_ref).astype(o_ref.dtype)


def _attn_prompt_kernel(q_ref, k_ref, v_ref, lq1_ref, lk1_ref, lq2_ref, lk2_ref, g_ref, o_ref, *, tq, seq):
    h = pl.program_id(1)
    for i in range(seq // tq):
        _prompt_query_tile(i, h, q_ref, k_ref, v_ref, (lq1_ref, lk1_ref, lq2_ref, lk2_ref), g_ref, o_ref, tq=tq)


def _attn_prompt(q, k, v, lams, g, tq=512):
    b, s, _ = q.shape
    w = 2 * HEAD_DIM
    spec = pl.BlockSpec((None, s, w), lambda bi, h: (bi, 0, h))
    return pl.pallas_call(
        functools.partial(_attn_prompt_kernel, tq=min(tq, s), seq=s),
        grid=(b, N_HEADS),
        in_specs=[spec, spec, spec] + [_full((1, HEAD_DIM))] * 4 + [_full((1, V_DIM))],
        out_specs=pl.BlockSpec((None, s, V_DIM), lambda bi, h: (bi, 0, h)),
        out_shape=jax.ShapeDtypeStruct((b, s, N_HEADS * V_DIM), BF16),
        compiler_params=_cparams(("parallel", "parallel")),
        name="attn_prompt",
    )(q, k, v, *lams, g)


NEW_PAD = 128


def _attn_sample_kernel(pt_ref, q_ref, kn_ref, vn_ref, bias_ref, off_ref, biasn_ref, *refs, pp, n_new, n_steps,
                        prompt_tiles):
    k_refs = refs[:pp]
    v_refs = refs[pp:2 * pp]
    lq1_ref, lk1_ref, lq2_ref, lk2_ref, g_ref = refs[2 * pp:2 * pp + 5]
    rest = refs[2 * pp + 5:]
    if prompt_tiles is not None:
        qp_ref, kp_ref, vp_ref, o_ref, op_ref, m_ref, l_ref, acc_ref = rest
        per_unit, tq, total = prompt_tiles
        flat = pl.program_id(0) * n_steps + pl.program_id(1)
        head = (flat // per_unit) % N_HEADS
        for i in range(per_unit):
            @pl.when((flat < total) & (flat % per_unit == i))
            def _(i=i):
                _prompt_query_tile(i, head, qp_ref, kp_ref, vp_ref, (lq1_ref, lk1_ref, lq2_ref, lk2_ref), g_ref,
                                   op_ref, tq=tq)
    else:
        o_ref, m_ref, l_ref, acc_ref = rest
    step = pl.program_id(1)
    q = q_ref[...]

    @pl.when(step == 0)
    def _():
        m_ref[...] = jnp.full(m_ref.shape, NEG_BIG, F32)
        l_ref[...] = jnp.zeros(l_ref.shape, F32)
        acc_ref[...] = jnp.zeros(acc_ref.shape, F32)

    def update(c, s, off, v):
        m_prev = m_ref[c][:, :1]
        m_new = jnp.maximum(m_prev, jnp.max(s, axis=-1, keepdims=True) + off)
        alpha = jnp.exp2(m_prev - m_new)
        p = jnp.exp2(s - (m_new - off))
        l_new = alpha * l_ref[c][:, :1] + jnp.sum(p, axis=-1, keepdims=True)
        acc_ref[c] = alpha * acc_ref[c] + jnp.dot(p.astype(BF16), v, preferred_element_type=F32)
        m_ref[c] = jnp.broadcast_to(m_new, m_ref.shape[1:])
        l_ref[c] = jnp.broadcast_to(l_new, l_ref.shape[1:])

    scores = [_nt_dot(q, k_refs[j][...].astype(BF16)) + bias_ref[...] for j in range(pp)]
    offs = [off_ref[:, :1] * (step * pp + j).astype(F32) for j in range(pp)]
    m_prevs = [m_ref[j][:, :1] for j in range(pp)]
    m_news = [jnp.maximum(m_prevs[j], jnp.max(scores[j], axis=-1, keepdims=True) + offs[j]) for j in range(pp)]
    ps = [jnp.exp2(scores[j] - (m_news[j] - offs[j])) for j in range(pp)]
    pvs = [jnp.dot(ps[j].astype(BF16), v_refs[j][...].astype(BF16), preferred_element_type=F32) for j in range(pp)]
    for j in range(pp):
        alpha = jnp.exp2(m_prevs[j] - m_news[j])
        l_new = alpha * l_ref[j][:, :1] + jnp.sum(ps[j], axis=-1, keepdims=True)
        acc_ref[j] = alpha * acc_ref[j] + pvs[j]
        m_ref[j] = jnp.broadcast_to(m_news[j], m_ref.shape[1:])
        l_ref[j] = jnp.broadcast_to(l_new, l_ref.shape[1:])

    @pl.when(step == n_steps - 1)
    def _():
        update(0, _nt_dot(q, kn_ref[...]) + biasn_ref[...], 0.0, vn_ref[...])
        m_all = m_ref[0][:, :1]
        for c in range(1, pp):
            m_all = jnp.maximum(m_all, m_ref[c][:, :1])
        l_all = jnp.zeros_like(m_all)
        acc_all = jnp.zeros(acc_ref.shape[1:], F32)
        for c in range(pp):
            w = jnp.exp2(m_ref[c][:, :1] - m_all)
            l_all = l_all + w * l_ref[c][:, :1]
            acc_all = acc_all + w * acc_ref[c]
        lam = _lambda_value(lq1_ref, lk1_ref, lq2_ref, lk2_ref)
        a = acc_all / l_all
        for h in range(N_HEADS):
            r0 = h * 2 * n_new
            o = a[r0:r0 + n_new] - lam * a[r0 + n_new:r0 + 2 * n_new]
            o_ref[:, h * V_DIM:(h + 1) * V_DIM] = _head_norm(o, g_ref).astype(o_ref.dtype)


def _attn_sample(q_all, kn, vn, cache_k, cache_v, page_table, lams, g, prompt=None, pp=8):
    db, n_pages = page_table.shape
    rows = q_all.shape[1]
    n_new = rows // (2 * N_HEADS)
    n_steps = n_pages // pp
    pr = PAGE_SIZE * N_HEADS

    slopes = 2.0 ** (-(ALIBI_MAX_EXP / N_HEADS) * (jnp.arange(rows, dtype=F32) // (2 * n_new) + 1.0))[:, None]
    row_head = (jnp.arange(rows, dtype=I32) // (2 * n_new))[:, None]
    row_q = (jnp.arange(rows, dtype=I32) % n_new)[:, None]
    col = jnp.arange(pr, dtype=I32)[None, :]
    bias = jnp.where(col % N_HEADS == row_head, slopes * LOG2E * (col // N_HEADS).astype(F32), -jnp.inf)
    off = jnp.broadcast_to(slopes * LOG2E * PAGE_SIZE, (rows, LANES))
    coln = jnp.arange(NEW_PAD, dtype=I32)[None, :]
    new_ok = (coln % N_HEADS == row_head) & (coln // N_HEADS <= row_q)
    biasn = jnp.where(new_ok, slopes * LOG2E * (n_pages * PAGE_SIZE + coln // N_HEADS).astype(F32), -jnp.inf)

    def page_spec(j):
        return pl.BlockSpec((None, pr, V_DIM), lambda b, s, pt: (pt[b * n_pages + s * pp + j], 0, 0))

    const = lambda shape: pl.BlockSpec(shape, lambda b, s, pt: (0,) * len(shape))
    per_seq = lambda r: pl.BlockSpec((None, r, V_DIM), lambda b, s, pt: (b, 0, 0))
    in_specs = ([per_seq(rows), per_seq(NEW_PAD), per_seq(NEW_PAD),
                 const((rows, pr)), const((rows, LANES)), const((rows, NEW_PAD))]
                + [page_spec(j) for j in range(pp)] + [page_spec(j) for j in range(pp)]
                + [const((1, HEAD_DIM))] * 4 + [const((1, V_DIM))])
    out_specs = [pl.BlockSpec((None, n_new, N_HEADS * V_DIM), lambda b, s, pt: (b, 0, 0))]
    out_shape = [jax.ShapeDtypeStruct((db, n_new, N_HEADS * V_DIM), BF16)]
    args = [q_all, kn, vn, bias, off, biasn, *([cache_k] * pp), *([cache_v] * pp), *lams, g]
    prompt_tiles = None
    if prompt is not None:
        qp, kp, vp, tq = prompt
        bsz, seq, _ = qp.shape
        per_unit = seq // tq
        units = bsz * N_HEADS
        prompt_tiles = (per_unit, tq, units * per_unit)
        assert db * n_steps >= units * per_unit, "not enough cache steps to carry the prompt tiles"

        def unit_map(b, s, pt):
            unit = jnp.minimum((b * n_steps + s) // per_unit, units - 1)
            return (unit // N_HEADS, 0, unit % N_HEADS)

        unit_spec = pl.BlockSpec((None, seq, V_DIM), unit_map)
        in_specs += [unit_spec] * 3
        out_specs.append(unit_spec)
        out_shape.append(jax.ShapeDtypeStruct((bsz, seq, N_HEADS * V_DIM), BF16))
        args += [qp, kp, vp]
    grid_spec = pltpu.PrefetchScalarGridSpec(
        num_scalar_prefetch=1,
        grid=(db, n_steps),
        in_specs=in_specs,
        out_specs=out_specs,
        scratch_shapes=[pltpu.VMEM((pp, rows, LANES), F32), pltpu.VMEM((pp, rows, LANES), F32),
                        pltpu.VMEM((pp, rows, V_DIM), F32)],
    )
    return pl.pallas_call(
        functools.partial(_attn_sample_kernel, pp=pp, n_new=n_new, n_steps=n_steps, prompt_tiles=prompt_tiles),
        grid_spec=grid_spec,
        out_shape=out_shape,
        compiler_params=_cparams(("arbitrary", "arbitrary")),
        name="attn_sample",
    )(page_table.reshape(-1), *args)


def _route(logits):
    lane = lax.broadcasted_iota(I32, logits.shape, 1)
    lane_f = lane.astype(F32)
    big = float(LANES)
    is_grp = lane < N_GROUPS
    lg = jnp.where(is_grp, logits, -jnp.inf)
    mg = jnp.max(lg, axis=-1, keepdims=True)
    g_sel = jnp.min(jnp.where(lg == mg, lane_f, big), axis=-1, keepdims=True)
    p_grp = 1.0 / jnp.sum(jnp.exp(lg - mg), axis=-1, keepdims=True)
    lo = N_GROUPS + EXPERTS_PER_GROUP * g_sel
    le = jnp.where((lane_f >= lo) & (lane_f < lo + EXPERTS_PER_GROUP), logits, -jnp.inf)
    m1 = jnp.max(le, axis=-1, keepdims=True)
    i1 = jnp.min(jnp.where(le == m1, lane_f, big), axis=-1, keepdims=True)
    le2 = jnp.where(lane_f == i1, -jnp.inf, le)
    m2 = jnp.max(le2, axis=-1, keepdims=True)
    i2 = jnp.min(jnp.where(le2 == m2, lane_f, big), axis=-1, keepdims=True)
    e2 = jnp.exp(m2 - m1)
    w1 = p_grp / (1.0 + e2)
    w2 = p_grp * e2 / (1.0 + e2)
    wts = jnp.where(lane == 0, w1, jnp.where(lane == 1, w2, 0.0))
    eid = jnp.where(lane == 0, i1 - N_GROUPS, jnp.where(lane == 1, i2 - N_GROUPS, 0.0)).astype(I32)
    return wts, eid


def _merge_kernel(x_ref, o_ref, conv_ref, sgc_ref, sga_ref, wa_ref, wo_ref, gf_ref, wr_ref, br_ref,
                  x1_ref, h2_ref, wts_ref, eid_ref):
    attn = jnp.dot(o_ref[...], wa_ref[...], preferred_element_type=F32)
    merged = sgc_ref[...].astype(F32) * conv_ref[...] + sga_ref[...].astype(F32) * attn
    x1 = x_ref[...] + jnp.dot(merged.astype(BF16), wo_ref[...], preferred_element_type=F32)
    x1_ref[...] = x1
    inv = lax.rsqrt(jnp.mean(x1 * x1, axis=-1, keepdims=True) + RMS_EPS)
    h2 = x1 * inv * gf_ref[...]
    h2_ref[...] = h2
    logits = jnp.dot(h2.astype(BF16), wr_ref[...], preferred_element_type=F32) + br_ref[...]
    wts, eid = _route(logits)
    wts_ref[...] = wts
    eid_ref[...] = eid


def _merge(x, o, conv, sgc, sga, wa, wo, gf, wr, br, tm=256):
    t, d = x.shape
    tm = min(tm, t)
    row = lambda w: pl.BlockSpec((tm, w), lambda i: (i, 0))
    single = lambda shape: pl.BlockSpec(shape, lambda i: (0,) * len(shape), pipeline_mode=pl.Buffered(1))
    return pl.pallas_call(
        _merge_kernel,
        grid=(t // tm,),
        in_specs=[row(d), row(d), row(d), row(d), row(d), single((d, d)), single((d, d)),
                  _full((1, d)), _full((d, LANES)), _full((1, LANES))],
        out_specs=[row(d), row(d), row(LANES), row(LANES)],
        out_shape=[jax.ShapeDtypeStruct((t, d), F32), jax.ShapeDtypeStruct((t, d), F32),
                   jax.ShapeDtypeStruct((t, LANES), F32), jax.ShapeDtypeStruct((t, LANES), I32)],
        compiler_params=_cparams(("parallel",)),
        name="merge_route",
    )(x, o, conv, sgc, sga, wa, wo, gf, wr, br)


def _rank_kernel(eid_ref, rank_ref, cnt_ref, carry_ref, *, tr):
    @pl.when(pl.program_id(0) == 0)
    def _():
        carry_ref[...] = jnp.zeros(carry_ref.shape, F32)

    eid = eid_ref[...]
    lane = lax.broadcasted_iota(I32, eid.shape, 1)
    e1 = jnp.sum(jnp.where(lane == 0, eid, 0), axis=-1, keepdims=True)
    e2 = jnp.sum(jnp.where(lane == 1, eid, 0), axis=-1, keepdims=True)
    hit1 = lane == e1
    hit2 = lane == e2
    onehot = jnp.where(hit1 | hit2, 1.0, 0.0)
    r = lax.broadcasted_iota(I32, (tr, tr), 0)
    c = lax.broadcasted_iota(I32, (tr, tr), 1)
    below = jnp.where(c < r, 1.0, 0.0).astype(BF16)
    carry = carry_ref[0:1, :]
    prefix = jnp.dot(below, onehot.astype(BF16), preferred_element_type=F32) + carry
    r1 = jnp.sum(jnp.where(hit1, prefix, 0.0), axis=-1, keepdims=True)
    r2 = jnp.sum(jnp.where(hit2, prefix, 0.0), axis=-1, keepdims=True)
    rank_ref[...] = jnp.where(lane == 0, r1, jnp.where(lane == 1, r2, 0.0)).astype(I32)
    total = carry + jnp.sum(onehot, axis=0, keepdims=True)
    carry_ref[...] = jnp.broadcast_to(total, carry_ref.shape)
    cnt_ref[...] = jnp.broadcast_to(total, cnt_ref.shape)


def _rank(eid, tr):
    t = eid.shape[0]
    return pl.pallas_call(
        functools.partial(_rank_kernel, tr=tr),
        grid=(t // tr,),
        in_specs=[pl.BlockSpec((tr, LANES), lambda i: (i, 0))],
        out_specs=[pl.BlockSpec((tr, LANES), lambda i: (i, 0)), _full((8, LANES))],
        out_shape=[jax.ShapeDtypeStruct((t, LANES), I32), jax.ShapeDtypeStruct((8, LANES), F32)],
        scratch_shapes=[pltpu.VMEM((8, LANES), F32)],
        compiler_params=_cparams(("arbitrary",)),
        name="expert_rank",
    )(eid)


def _tile_slots(pos, rows):
    t = pos.shape[1]
    return pos.reshape(2, t // rows, rows).transpose(1, 0, 2).reshape(t // rows, 1, 2 * rows)


def _dispatch_kernel(pos_ref, h_ref, xs_in_ref, xs_ref, sem, *, td):
    del xs_in_ref

    def row_copy(r, k):
        p = pos_ref[0, 0, k * td + r]
        return pltpu.make_async_copy(h_ref.at[pl.ds(r, 1)], xs_ref.at[pl.ds(p, 1)], sem)

    def start(r, carry):
        row_copy(r, 0).start()
        row_copy(r, 1).start(priority=1)
        return carry

    lax.fori_loop(0, td, start, 0, unroll=DMA_UNROLL)
    for _ in range(2):
        pltpu.make_async_copy(h_ref, xs_ref.at[pl.ds(0, td)], sem).wait()


def _dispatch(h2, pos, xs0, td):
    t, d = h2.shape
    pos3 = _tile_slots(pos, td)
    return pl.pallas_call(
        functools.partial(_dispatch_kernel, td=td),
        grid=(t // td,),
        in_specs=[pl.BlockSpec((1, 1, 2 * td), lambda i: (i, 0, 0), memory_space=pltpu.SMEM),
                  pl.BlockSpec((td, d), lambda i: (i, 0)),
                  pl.BlockSpec(memory_space=pl.ANY)],
        out_specs=pl.BlockSpec(memory_space=pl.ANY),
        out_shape=jax.ShapeDtypeStruct(xs0.shape, F32),
        scratch_shapes=[pltpu.SemaphoreType.DMA(())],
        input_output_aliases={2: 0},
        compiler_params=_cparams(("arbitrary",)),
        name="moe_dispatch",
    )(pos3, h2, xs0)


def _experts_kernel(te_ref, nu_ref, x_ref, wg_ref, wu_ref, wd_ref, y_ref, wgb_ref, wub_ref, wdb_ref):
    i = pl.program_id(0)
    active = i < nu_ref[0]
    new_expert = (i == 0) | (te_ref[i] != te_ref[jnp.maximum(i - 1, 0)])

    @pl.when(active & new_expert)
    def _():
        wgb_ref[...] = wg_ref[...].astype(BF16)
        wub_ref[...] = wu_ref[...].astype(BF16)
        wdb_ref[...] = wd_ref[...].astype(BF16)

    @pl.when(active)
    def _():
        x = x_ref[...].astype(BF16)
        hg = jnp.dot(x, wgb_ref[...], preferred_element_type=F32)
        hu = jnp.dot(x, wub_ref[...], preferred_element_type=F32)
        act = hg * jax.nn.sigmoid(hg) * hu
        y_ref[...] = jnp.dot(act.astype(BF16), wdb_ref[...], preferred_element_type=F32)

    @pl.when(pl.program_id(0) >= nu_ref[0])
    def _():
        y_ref[...] = jnp.zeros(y_ref.shape, F32)


def _experts(xs, tile_expert, n_used, wg, wu, wd, tme):
    n_rows, d = xs.shape
    f = wg.shape[2]
    n_tiles = n_rows // tme
    tile = lambda i, te, nu: (jnp.minimum(i, nu[0] - 1), 0)
    grid_spec = pltpu.PrefetchScalarGridSpec(
        num_scalar_prefetch=2,
        grid=(n_tiles,),
        in_specs=[pl.BlockSpec((tme, d), tile),
                  pl.BlockSpec((None, d, f), lambda i, te, nu: (te[i], 0, 0)),
                  pl.BlockSpec((None, d, f), lambda i, te, nu: (te[i], 0, 0)),
                  pl.BlockSpec((None, f, d), lambda i, te, nu: (te[i], 0, 0))],
        out_specs=pl.BlockSpec((tme, d), lambda i, te, nu: (i, 0)),
        scratch_shapes=[pltpu.VMEM((d, f), BF16), pltpu.VMEM((d, f), BF16), pltpu.VMEM((f, d), BF16)],
    )
    return pl.pallas_call(
        _experts_kernel,
        grid_spec=grid_spec,
        out_shape=jax.ShapeDtypeStruct((n_rows, d), F32),
        compiler_params=_cparams(("arbitrary",)),
        name="moe_experts",
    )(tile_expert, n_used, xs, wg, wu, wd)


def _combine_kernel(pos_ref, x1_ref, wts_ref, gfin_ref, y_ref, o_ref, buf_ref, sem, *, tc):
    def row_copy(r, k):
        p = pos_ref[0, 0, k * tc + r]
        return pltpu.make_async_copy(y_ref.at[pl.ds(p, 1)], buf_ref.at[k, pl.ds(r, 1)], sem)

    def start(r, carry):
        row_copy(r, 0).start()
        row_copy(r, 1).start(priority=1)
        return carry

    lax.fori_loop(0, tc, start, 0, unroll=DMA_UNROLL)
    for k in range(2):
        pltpu.make_async_copy(y_ref.at[pl.ds(0, tc)], buf_ref.at[k], sem).wait()
    wts = wts_ref[...]
    x2 = x1_ref[...] + wts[:, 0:1] * buf_ref[0] + wts[:, 1:2] * buf_ref[1]
    inv = lax.rsqrt(jnp.mean(x2 * x2, axis=-1, keepdims=True) + RMS_EPS)
    o_ref[...] = x2 * inv * gfin_ref[...]


def _combine(x1, wts, pos, y, gfin, tc):
    t, d = x1.shape
    pos3 = _tile_slots(pos, tc)
    return pl.pallas_call(
        functools.partial(_combine_kernel, tc=tc),
        grid=(t // tc,),
        in_specs=[pl.BlockSpec((1, 1, 2 * tc), lambda i: (i, 0, 0), memory_space=pltpu.SMEM),
                  pl.BlockSpec((tc, d), lambda i: (i, 0)),
                  pl.BlockSpec((tc, LANES), lambda i: (i, 0)),
                  _full((1, d)),
                  pl.BlockSpec(memory_space=pl.ANY)],
        out_specs=pl.BlockSpec((tc, d), lambda i: (i, 0)),
        out_shape=jax.ShapeDtypeStruct((t, d), F32),
        scratch_shapes=[pltpu.VMEM((2, tc, d), F32), pltpu.SemaphoreType.DMA(())],
        compiler_params=_cparams(("arbitrary",)),
        name="moe_combine",
    )(pos3, x1, wts, gfin, y)


def _moe(groups, wg, wu, wd, gfin, tme, tr, td):
    sizes = [g[0].shape[0] for g in groups]
    t = sum(sizes)
    eid = jnp.concatenate([g[3] for g in groups], axis=0)
    rank, cnt = _rank(eid, min(tr, t))
    counts = cnt[0, :N_EXPERTS].astype(I32)
    padded = ((counts + tme - 1) // tme) * tme
    ends = jnp.cumsum(padded)
    offsets = ends - padded
    pos = (offsets[eid[:, :2].T] + rank[:, :2].T).astype(I32)
    n_tiles = (2 * t + N_EXPERTS * (tme - 1)) // tme
    tile_start = jnp.arange(n_tiles, dtype=I32) * tme
    tile_expert = jnp.minimum(
        jnp.sum((ends[None, :] <= tile_start[:, None]).astype(I32), axis=1), N_EXPERTS - 1).astype(I32)
    n_used = (ends[-1:] // tme).astype(I32)
    starts =[sum(sizes[:i]) for i in range(len(sizes))]
    xs = jnp.zeros((n_tiles * tme, groups[0][1].shape[1]), F32)
    for (x1, h2, wts, _), s0, n in zip(groups, starts, sizes):
        xs = _dispatch(h2, pos[:, s0:s0 + n], xs, min(td, n))
    y = _experts(xs, tile_expert, n_used, wg, wu, wd, tme)
    return [_combine(x1, wts, pos[:, s0:s0 + n], y, gfin, min(td, n))
            for (x1, h2, wts, _), s0, n in zip(groups, starts, sizes)]


def kernel(x_prompt, x_sample, cache_k, cache_v, state_conv, page_table, norm_mix_g, w_in, conv_dw_w, conv_dw_b, conv_ln_g, conv_ln_b, conv_w_out, lambda_q1, lambda_k1, lambda_q2, lambda_k2, head_norm_g, attn_w_out, w_out, norm_ffn_g, router_group_w, router_group_b, router_expert_w, router_expert_b, expert_w_gate, expert_w_up, expert_w_down, final_norm_g):
    bsz, seq, d = x_prompt.shape
    db, ds, _ = x_sample.shape
    l = LAYER
    qkw = 2 * N_HEADS * HEAD_DIM
    vw = N_HEADS * V_DIM

    win = w_in[l]
    bounds = [0, D_CONV, 2 * D_CONV, 2 * D_CONV + qkw, 2 * D_CONV + 2 * qkw, 2 * D_CONV + 2 * qkw + vw,
              2 * D_CONV + 2 * qkw + vw + d, 2 * D_CONV + 2 * qkw + vw + 2 * d]
    w_all = jnp.concatenate([win[:, bounds[1]:bounds[2]], win[:, bounds[0]:bounds[1]], win[:, bounds[2]:]],
                            axis=1).astype(BF16)
    w_pw = conv_w_out[l].astype(BF16)
    w_ao = attn_w_out[l].astype(BF16)
    w_o = w_out[l].astype(BF16)
    w_r = jnp.zeros((d, LANES), F32).at[:, :N_GROUPS].set(router_group_w[l])
    w_r = w_r.at[:, N_GROUPS:N_GROUPS + N_EXPERTS].set(router_expert_w[l]).astype(BF16)
    b_r = jnp.zeros((1, LANES), F32).at[0, :N_GROUPS].set(router_group_b[l])
    b_r = b_r.at[0, N_GROUPS:N_GROUPS + N_EXPERTS].set(router_expert_b[l])
    wg, wu, wd = expert_w_gate[l], expert_w_up[l], expert_w_down[l]
    row = lambda a: a.reshape(1, -1)
    g_mix, g_ffn, g_fin = row(norm_mix_g[l]), row(norm_ffn_g[l]), row(final_norm_g)
    dww, dwb = conv_dw_w[l], row(conv_dw_b[l])
    lng, lnb = row(conv_ln_g[l]), row(conv_ln_b[l])
    lams = [row(lambda_q1[l]), row(lambda_k1[l]), row(lambda_q2[l]), row(lambda_k2[l])]
    g_head = row(head_norm_g[l])

    def project(x2d, tm):
        return _project(x2d, g_mix, w_all, tm)

    def merge(x2d, o, conv, sgc, sga):
        return _merge(x2d, o, conv, sgc, sga, w_ao, w_o, g_ffn, w_r, b_r)

    tp = bsz * seq
    xp = x_prompt.reshape(tp, d)
    u_p, q_p, k_p, kb_p, v_p, vb_p, sgc_p, sga_p = project(xp, 1024)
    conv_p = _conv_prompt(u_p.reshape(bsz, seq, D_CONV), dww, dwb, lng, lnb, w_pw).reshape(tp, d)
    qkv_p = (q_p.reshape(bsz, seq, qkw), kb_p.reshape(bsz, seq, qkw), vb_p.reshape(bsz, seq, vw))

    tsamp = db * ds
    xs_ = x_sample.reshape(tsamp, d)
    u_s, q_s, k_s, kb_s, v_s, vb_s, sgc_s, sga_s = project(xs_, tsamp)
    u_ext = jnp.concatenate([state_conv[l], u_s.reshape(db, ds, D_CONV)], axis=1)
    conv_s = _conv_sample(jnp.swapaxes(u_ext, 0, 1), dww, dwb, lng, lnb, w_pw)
    conv_s = jnp.swapaxes(conv_s, 0, 1).reshape(tsamp, d)
    q5 = q_s.reshape(db, ds, N_HEADS, 2, HEAD_DIM).transpose(0, 2, 3, 1, 4)
    zero = jnp.zeros_like(q5[:, :, 0])
    qbd = jnp.stack([jnp.concatenate([q5[:, :, 0], zero], axis=-1),
                     jnp.concatenate([zero, q5[:, :, 1]], axis=-1)], axis=2).reshape(db, N_HEADS * 2 * ds, V_DIM)
    pad_rows = NEW_PAD - ds * N_HEADS
    kn = jnp.pad(kb_s.reshape(db, ds * N_HEADS, V_DIM), ((0, 0), (0, pad_rows), (0, 0)))
    vn = jnp.pad(vb_s.reshape(db, ds * N_HEADS, V_DIM), ((0, 0), (0, pad_rows), (0, 0)))
    n_pool = cache_k.shape[1]
    ck = cache_k.reshape(cache_k.shape[0] * n_pool, PAGE_SIZE * N_HEADS, V_DIM)
    cv = cache_v.reshape(cache_v.shape[0] * n_pool, PAGE_SIZE * N_HEADS, V_DIM)
    tq = min(PROMPT_TQ, seq)
    pt = page_table + l * n_pool
    if db * (page_table.shape[1] // ATTN_PAGES) >= bsz * N_HEADS * (seq // tq):
        o_s, o_p = _attn_sample(qbd, kn, vn, ck, cv, pt, lams, g_head, prompt=(*qkv_p, tq), pp=ATTN_PAGES)
    else:
        (o_s,) = _attn_sample(qbd, kn, vn, ck, cv, pt, lams, g_head, pp=ATTN_PAGES)
        o_p = _attn_prompt(*qkv_p, lams, g_head, tq=tq)
    group_p = merge(xp, o_p.reshape(tp, vw), conv_p, sgc_p, sga_p)
    group_s = merge(xs_, o_s.reshape(tsamp, vw), conv_s, sgc_s, sga_s)

    t_all = tp + tsamp
    tr = next(c for c in (512, 256, 128) if t_all % c == 0)
    y_p, y_s = _moe([group_p, group_s], wg, wu, wd, g_fin, MOE_TILE, tr, 256)
    y_p = y_p.reshape(bsz, seq, d)
    y_s = y_s.reshape(db, ds, d)

    k_prompt = k_p.reshape(1, bsz, seq, N_HEADS, 2 * HEAD_DIM)
    v_prompt = v_p.reshape(1, bsz, seq, N_HEADS, V_DIM)
    conv_prompt = u_p.reshape(bsz, seq, D_CONV)[:, seq - (CONV_W - 1):][None]
    k_sample = k_s.reshape(1, db, ds, N_HEADS, 2 * HEAD_DIM)
    v_sample = v_s.reshape(1, db, ds, N_HEADS, V_DIM)
    conv_sample = u_ext[:, ds:][None]
    return (y_p, y_s, k_prompt, v_prompt, conv_prompt, k_sample, v_sample, conv_sample)
```

```python
import functools
import math

import jax
import jax.numpy as jnp
from jax import lax
from jax.experimental import pallas as pl
from jax.experimental.pallas import tpu as pltpu

F32 = jnp.float32
BF16 = jnp.bfloat16
I32 = jnp.int32

D_MODEL = 2048
HEAD_DIM = 128
N_HEADS = D_MODEL // (2 * HEAD_DIM)
V_DIM = 2 * HEAD_DIM
D_CONV = D_MODEL // 2
CONV_W = 31
N_GROUPS = 4
EXPERTS_PER_GROUP = 8
N_EXPERTS = N_GROUPS * EXPERTS_PER_GROUP
D_EXPERT = D_MODEL // 4
PAGE_SIZE = 128
RMS_EPS = 1e-6
LN_EPS = 1e-5
ALIBI_MAX_EXP = 8.0
LAYER = 0
LAM_INIT = 0.8 - 0.6 * math.exp(-0.3 * LAYER)
LOG2E = math.log2(math.e)

LANES = 128
SUBLANES = 8
NEG_BIG = -1e30
PROMPT_TQ = 512
ATTN_PAGES = 8
MOE_TILE = 512
DMA_UNROLL = 8
VMEM_LIMIT = 56 * 1024 * 1024


def _cparams(sem):
    return pltpu.CompilerParams(dimension_semantics=sem, vmem_limit_bytes=VMEM_LIMIT)


def _full(shape):
    return pl.BlockSpec(shape, lambda *_: (0,) * len(shape))


PROJ_SEGMENTS = (("glu_b", D_CONV), ("glu_a", D_CONV), ("q", D_MODEL), ("k", D_MODEL), ("v", D_MODEL),
                 ("g_conv", D_MODEL), ("g_attn", D_MODEL))


def _proj_kernel(x_ref, g_ref, w_ref, u_ref, q_ref, k_ref, kb_ref, v_ref, vb_ref, gc_ref, ga_ref, h_ref, sig_ref,
                 *, tn):
    j = pl.program_id(1)

    @pl.when(j == 0)
    def _():
        x = x_ref[...]
        inv = lax.rsqrt(jnp.mean(x * x, axis=-1, keepdims=True) + RMS_EPS)
        h_ref[...] = (x * inv * g_ref[...]).astype(BF16)

    z = jnp.dot(h_ref[...], w_ref[...], preferred_element_type=F32)

    def glu_b(jj):
        sig_ref[jj] = jax.nn.sigmoid(z)

    def glu_a(jj):
        u_ref[...] = z * sig_ref[jj]

    def q_seg(jj):
        q_ref[...] = (z * (LOG2E * HEAD_DIM ** -0.5)).astype(q_ref.dtype)

    def k_seg(jj):
        k_ref[...] = z
        kb_ref[...] = z.astype(kb_ref.dtype)

    def v_seg(jj):
        v_ref[...] = z
        vb_ref[...] = z.astype(vb_ref.dtype)

    def gc_seg(jj):
        gc_ref[...] = jax.nn.sigmoid(z).astype(gc_ref.dtype)

    def ga_seg(jj):
        ga_ref[...] = jax.nn.sigmoid(z).astype(ga_ref.dtype)

    start = 0
    for (_, width), fn in zip(PROJ_SEGMENTS, (glu_b, glu_a, q_seg, k_seg, v_seg, gc_seg, ga_seg)):
        blocks = width // tn
        if fn in (glu_b, glu_a):
            for jj in range(blocks):
                pl.when(j == start + jj)(functools.partial(fn, jj))
        else:
            pl.when((j >= start) & (j < start + blocks))(functools.partial(fn, 0))
        start += blocks


def _project(x, g, w_all, tm, tn=512):
    t, d = x.shape
    n = w_all.shape[1]
    starts, s0 = {}, 0
    for name, width in PROJ_SEGMENTS:
        starts[name] = (s0 // tn, width // tn)
        s0 += width

    def seg_spec(name):
        first, blocks = starts[name]
        return pl.BlockSpec((tm, tn), lambda i, j: (i, jnp.clip(j - first, 0, blocks - 1)))

    outs = [("glu_a", D_CONV, F32), ("q", D_MODEL, BF16), ("k", D_MODEL, F32), ("k", D_MODEL, BF16),
            ("v", D_MODEL, F32), ("v", D_MODEL, BF16), ("g_conv", D_MODEL, BF16), ("g_attn", D_MODEL, BF16)]
    return pl.pallas_call(
        functools.partial(_proj_kernel, tn=tn),
        grid=(t // tm, n // tn),
        in_specs=[pl.BlockSpec((tm, d), lambda i, j: (i, 0)), _full((1, d)),
                  pl.BlockSpec((d, tn), lambda i, j: (0, j))],
        out_specs=[seg_spec(name) for name, _, _ in outs],
        out_shape=[jax.ShapeDtypeStruct((t, width), dt) for _, width, dt in outs],
        scratch_shapes=[pltpu.VMEM((tm, d), BF16), pltpu.VMEM((D_CONV // tn, tm, tn), F32)],
        compiler_params=_cparams(("arbitrary", "arbitrary")),
        name="proj",
    )(x, g, w_all)


def _ln_swish_project(z, lng_ref, lnb_ref, wpw_ref):
    mu = jnp.mean(z, axis=-1, keepdims=True)
    zc = z - mu
    var = jnp.mean(zc * zc, axis=-1, keepdims=True)
    y = zc * lax.rsqrt(var + LN_EPS) * lng_ref[...] + lnb_ref[...]
    y = y * jax.nn.sigmoid(y)
    return jnp.dot(y.astype(BF16), wpw_ref[...], preferred_element_type=F32)


HALO = 32
CONV_ROWS = 128


def _conv_prompt_kernel(u_ref, halo_ref, dww_ref, dwb_ref, lng_ref, lnb_ref, wpw_ref, o_ref, ext_ref, z_ref, sh_ref,
                        *, ts):
    first = pl.program_id(1) == 0
    ext_ref[0:HALO, :] = jnp.where(first, 0.0, halo_ref[...])
    ext_ref[HALO:, :] = u_ref[...]
    base = HALO - (CONV_W - 1)
    for cb in range(D_CONV // LANES):
        cs = slice(cb * LANES, (cb + 1) * LANES)
        for shift in range(SUBLANES):
            span = ts + HALO - (SUBLANES if shift else 0)
            sh_ref[shift, 0:span, :] = ext_ref[shift:shift + span, cs]
        for r0 in range(0, ts, CONV_ROWS):
            acc = jnp.broadcast_to(dwb_ref[:, cs], (CONV_ROWS, LANES))
            for j in range(CONV_W):
                shift = (base + j) % SUBLANES
                a = base + j - shift + r0
                acc = acc + dww_ref[j:j + 1, cs] * sh_ref[shift, a:a + CONV_ROWS, :]
            z_ref[r0:r0 + CONV_ROWS, cs] = acc
    o_ref[...] = _ln_swish_project(z_ref[...], lng_ref, lnb_ref, wpw_ref)


def _conv_prompt(u, dww, dwb, lng, lnb, wpw, ts=256):
    b, s, c = u.shape
    d = wpw.shape[1]
    r = ts // HALO
    return pl.pallas_call(
        functools.partial(_conv_prompt_kernel, ts=ts),
        grid=(b, s // ts),
        in_specs=[
            pl.BlockSpec((None, ts, c), lambda bi, i: (bi, i, 0)),
            pl.BlockSpec((None, HALO, c), lambda bi, i: (bi, jnp.maximum(i * r - 1, 0), 0)),
            _full((CONV_W, c)), _full((1, c)), _full((1, c)), _full((1, c)), _full((c, d)),
        ],
        out_specs=pl.BlockSpec((None, ts, d), lambda bi, i: (bi, i, 0)),
        out_shape=jax.ShapeDtypeStruct((b, s, d), F32),
        scratch_shapes=[pltpu.VMEM((ts + HALO, c), F32), pltpu.VMEM((ts, c), F32),
                        pltpu.VMEM((SUBLANES, ts + HALO, LANES), F32)],
        compiler_params=_cparams(("parallel", "arbitrary")),
        name="conv_prompt",
    )(u, u, dww, dwb, lng, lnb, wpw)


def _conv_sample_kernel(ext_ref, dww_ref, dwb_ref, lng_ref, lnb_ref, wpw_ref, o_ref, *, n_new, bt):
    zs = []
    for t in range(n_new):
        acc = jnp.broadcast_to(dwb_ref[...], (bt, D_CONV))
        for j in range(CONV_W):
            acc = acc + dww_ref[j:j + 1, :] * ext_ref[t + j]
        zs.append(acc)
    out = _ln_swish_project(jnp.concatenate(zs, axis=0), lng_ref, lnb_ref, wpw_ref)
    for t in range(n_new):
        o_ref[t] = out[t * bt:(t + 1) * bt]


def _conv_sample(ext_t, dww, dwb, lng, lnb, wpw, bt=32):
    rows, b, c = ext_t.shape
    n_new = rows - (CONV_W - 1)
    d = wpw.shape[1]
    return pl.pallas_call(
        functools.partial(_conv_sample_kernel, n_new=n_new, bt=bt),
        grid=(b // bt,),
        in_specs=[
            pl.BlockSpec((rows, bt, c), lambda i: (0, i, 0)),
            _full((CONV_W, c)), _full((1, c)), _full((1, c)), _full((1, c)), _full((c, d)),
        ],
        out_specs=pl.BlockSpec((n_new, bt, d), lambda i: (0, i, 0)),
        out_shape=jax.ShapeDtypeStruct((n_new, b, d), F32),
        compiler_params=_cparams(("parallel",)),
        name="conv_sample",
    )(ext_t, dww, dwb, lng, lnb, wpw)


def _lambda_value(lq1_ref, lk1_ref, lq2_ref, lk2_ref):
    a = jnp.sum(lq1_ref[...] * lk1_ref[...], axis=-1, keepdims=True)
    b = jnp.sum(lq2_ref[...] * lk2_ref[...], axis=-1, keepdims=True)
    return jnp.exp(a) - jnp.exp(b) + LAM_INIT


def _head_norm(o, g_ref):
    inv = lax.rsqrt(jnp.mean(o * o, axis=-1, keepdims=True) + RMS_EPS)
    return o * inv * g_ref[...] * (1.0 - LAM_INIT)


def _alibi_slope(h, shape):
    return jnp.exp2(jnp.full(shape, -(ALIBI_MAX_EXP / N_HEADS), F32) * (h + 1).astype(F32))


def _nt_dot(a, b):
    return lax.dot_general(a, b, (((1,), (1,)), ((), ())), preferred_element_type=F32)


def _prompt_query_tile(i, h, q_ref, k_ref, v_ref, lam_refs, g_ref, o_ref, *, tq):
    lo, hi = i * tq, (i + 1) * tq
    kpos = lax.broadcasted_iota(I32, (1, hi), 1).astype(F32)
    bias = _alibi_slope(h, (1, hi)) * LOG2E * kpos
    lam = _lambda_value(*lam_refs)
    visible = lax.broadcasted_iota(I32, (tq, tq), 1) <= lax.broadcasted_iota(I32, (tq, tq), 0)
    outs = []
    for c in range(2):
        cs = slice(c * HEAD_DIM, (c + 1) * HEAD_DIM)
        q = q_ref[lo:hi, cs]
        s_d = jnp.where(visible, _nt_dot(q, k_ref[lo:hi, cs]) + bias[:, lo:hi], -jnp.inf)
        m = jnp.max(s_d, axis=-1, keepdims=True)
        if i > 0:
            s_o = _nt_dot(q, k_ref[0:lo, cs]) + bias[:, 0:lo]
            m = jnp.maximum(m, jnp.max(s_o, axis=-1, keepdims=True))
        p_d = jnp.exp2(s_d - m)
        l = jnp.sum(p_d, axis=-1, keepdims=True)
        pv = jnp.dot(p_d.astype(BF16), v_ref[lo:hi, :], preferred_element_type=F32)
        if i > 0:
            p_o = jnp.exp2(s_o - m)
            l = l + jnp.sum(p_o, axis=-1, keepdims=True)
            pv = pv + jnp.dot(p_o.astype(BF16), v_ref[0:lo, :], preferred_element_type=F32)
        outs.append(pv / l)
    o_ref[lo:hi, :] = _head_norm(outs[0] - lam * outs[1], g_ref).astype(o_ref.dtype)


def _attn_prompt_kernel(q_ref, k_ref, v_ref, lq1_ref, lk1_ref, lq2_ref, lk2_ref, g_ref, o_ref, *, tq, seq):
    h = pl.program_id(1)
    for i in range(seq // tq):
        _prompt_query_tile(i, h, q_ref, k_ref, v_ref, (lq1_ref, lk1_ref, lq2_ref, lk2_ref), g_ref, o_ref, tq=tq)


def _attn_prompt(q, k, v, lams, g, tq=512):
    b, s, _ = q.shape
    w = 2 * HEAD_DIM
    spec = pl.BlockSpec((None, s, w), lambda bi, h: (bi, 0, h))
    return pl.pallas_call(
        functools.partial(_attn_prompt_kernel, tq=min(tq, s), seq=s),
        grid=(b, N_HEADS),
        in_specs=[spec, spec, spec] + [_full((1, HEAD_DIM))] * 4 + [_full((1, V_DIM))],
        out_specs=pl.BlockSpec((None, s, V_DIM), lambda bi, h: (bi, 0, h)),
        out_shape=jax.ShapeDtypeStruct((b, s, N_HEADS * V_DIM), BF16),
        compiler_params=_cparams(("parallel", "parallel")),
        name="attn_prompt",
    )(q, k, v, *lams, g)


NEW_PAD = 128


def _attn_sample_kernel(pt_ref, q_ref, kn_ref, vn_ref, bias_ref, off_ref, biasn_ref, *refs, pp, n_new, n_steps,
                        prompt_tiles):
    k_refs = refs[:pp]
    v_refs = refs[pp:2 * pp]
    lq1_ref, lk1_ref, lq2_ref, lk2_ref, g_ref = refs[2 * pp:2 * pp + 5]
    rest = refs[2 * pp + 5:]
    if prompt_tiles is not None:
        qp_ref, kp_ref, vp_ref, o_ref, op_ref, m_ref, l_ref, acc_ref = rest
    else:
        o_ref, m_ref, l_ref, acc_ref = rest
    step = pl.program_id(1)
    q = q_ref[...]

    @pl.when(step == 0)
    def _():
        m_ref[...] = jnp.full(m_ref.shape, NEG_BIG, F32)
        l_ref[...] = jnp.zeros(l_ref.shape, F32)
        acc_ref[...] = jnp.zeros(acc_ref.shape, F32)

    def update(c, s, off, v):
        m_prev = m_ref[c][:, :1]
        m_new = jnp.maximum(m_prev, jnp.max(s, axis=-1, keepdims=True) + off)
        alpha = jnp.exp2(m_prev - m_new)
        p = jnp.exp2(s - (m_new - off))
        l_new = alpha * l_ref[c][:, :1] + jnp.sum(p, axis=-1, keepdims=True)
        acc_ref[c] = alpha * acc_ref[c] + jnp.dot(p.astype(BF16), v, preferred_element_type=F32)
        m_ref[c] = jnp.broadcast_to(m_new, m_ref.shape[1:])
        l_ref[c] = jnp.broadcast_to(l_new, l_ref.shape[1:])

    def stream_pages():
        scores = [_nt_dot(q, k_refs[j][...].astype(BF16)) + bias_ref[...] for j in range(pp)]
        offs = [off_ref[:, :1] * (step * pp + j).astype(F32) for j in range(pp)]
        m_prevs = [m_ref[j][:, :1] for j in range(pp)]
        m_news = [jnp.maximum(m_prevs[j], jnp.max(scores[j], axis=-1, keepdims=True) + offs[j]) for j in range(pp)]
        ps = [jnp.exp2(scores[j] - (m_news[j] - offs[j])) for j in range(pp)]
        pvs = [jnp.dot(ps[j].astype(BF16), v_refs[j][...].astype(BF16), preferred_element_type=F32)
               for j in range(pp)]
        for j in range(pp):
            alpha = jnp.exp2(m_prevs[j] - m_news[j])
            l_new = alpha * l_ref[j][:, :1] + jnp.sum(ps[j], axis=-1, keepdims=True)
            acc_ref[j] = alpha * acc_ref[j] + pvs[j]
            m_ref[j] = jnp.broadcast_to(m_news[j], m_ref.shape[1:])
            l_ref[j] = jnp.broadcast_to(l_new, l_ref.shape[1:])

    if prompt_tiles is None:
        stream_pages()
    else:
        per_unit, tq, total = prompt_tiles
        flat = pl.program_id(0) * n_steps + step
        head = (flat // per_unit) % N_HEADS
        for i in range(per_unit):
            @pl.when((flat < total) & (flat % per_unit == i))
            def _(i=i):
                _prompt_query_tile(i, head, qp_ref, kp_ref, vp_ref, (lq1_ref, lk1_ref, lq2_ref, lk2_ref), g_ref,
                                   op_ref, tq=tq)
                stream_pages()

        pl.when(flat >= total)(stream_pages)

    @pl.when(step == n_steps - 1)
    def _():
        update(0, _nt_dot(q, kn_ref[...]) + biasn_ref[...], 0.0, vn_ref[...])
        m_all = m_ref[0][:, :1]
        for c in range(1, pp):
            m_all = jnp.maximum(m_all, m_ref[c][:, :1])
        l_all = jnp.zeros_like(m_all)
        acc_all = jnp.zeros(acc_ref.shape[1:], F32)
        for c in range(pp):
            w = jnp.exp2(m_ref[c][:, :1] - m_all)
            l_all = l_all + w * l_ref[c][:, :1]
            acc_all = acc_all + w * acc_ref[c]
        lam = _lambda_value(lq1_ref, lk1_ref, lq2_ref, lk2_ref)
        a = acc_all / l_all
        for h in range(N_HEADS):
            r0 = h * 2 * n_new
            o = a[r0:r0 + n_new] - lam * a[r0 + n_new:r0 + 2 * n_new]
            o_ref[:, h * V_DIM:(h + 1) * V_DIM] = _head_norm(o, g_ref).astype(o_ref.dtype)


def _attn_sample(q_all, kn, vn, cache_k, cache_v, page_table, lams, g, prompt=None, pp=8):
    db, n_pages = page_table.shape
    rows = q_all.shape[1]
    n_new = rows // (2 * N_HEADS)
    n_steps = n_pages // pp
    pr = PAGE_SIZE * N_HEADS

    slopes = 2.0 ** (-(ALIBI_MAX_EXP / N_HEADS) * (jnp.arange(rows, dtype=F32) // (2 * n_new) + 1.0))[:, None]
    row_head = (jnp.arange(rows, dtype=I32) // (2 * n_new))[:, None]
    row_q = (jnp.arange(rows, dtype=I32) % n_new)[:, None]
    col = jnp.arange(pr, dtype=I32)[None, :]
    bias = jnp.where(col % N_HEADS == row_head, slopes * LOG2E * (col // N_HEADS).astype(F32), -jnp.inf)
    off = jnp.broadcast_to(slopes * LOG2E * PAGE_SIZE, (rows, LANES))
    coln = jnp.arange(NEW_PAD, dtype=I32)[None, :]
    new_ok = (coln % N_HEADS == row_head) & (coln // N_HEADS <= row_q)
    biasn = jnp.where(new_ok, slopes * LOG2E * (n_pages * PAGE_SIZE + coln // N_HEADS).astype(F32), -jnp.inf)

    def page_spec(j):
        return pl.BlockSpec((None, pr, V_DIM), lambda b, s, pt: (pt[b * n_pages + s * pp + j], 0, 0))

    const = lambda shape: pl.BlockSpec(shape, lambda b, s, pt: (0,) * len(shape))
    per_seq = lambda r: pl.BlockSpec((None, r, V_DIM), lambda b, s, pt: (b, 0, 0))
    in_specs = ([per_seq(rows), per_seq(NEW_PAD), per_seq(NEW_PAD),
                 const((rows, pr)), const((rows, LANES)), const((rows, NEW_PAD))]
                + [page_spec(j) for j in range(pp)] + [page_spec(j) for j in range(pp)]
                + [const((1, HEAD_DIM))] * 4 + [const((1, V_DIM))])
    out_specs = [pl.BlockSpec((None, n_new, N_HEADS * V_DIM), lambda b, s, pt: (b, 0, 0))]
    out_shape = [jax.ShapeDtypeStruct((db, n_new, N_HEADS * V_DIM), BF16)]
    args = [q_all, kn, vn, bias, off, biasn, *([cache_k] * pp), *([cache_v] * pp), *lams, g]
    prompt_tiles = None
    if prompt is not None:
        qp, kp, vp, tq = prompt
        bsz, seq, _ = qp.shape
        per_unit = seq // tq
        units = bsz * N_HEADS
        prompt_tiles = (per_unit, tq, units * per_unit)
        assert db * n_steps >= units * per_unit, "not enough cache steps to carry the prompt tiles"

        def unit_map(b, s, pt):
            unit = jnp.minimum((b * n_steps + s) // per_unit, units - 1)
            return (unit // N_HEADS, 0, unit % N_HEADS)

        unit_spec = pl.BlockSpec((None, seq, V_DIM), unit_map)
        in_specs += [unit_spec] * 3
        out_specs.append(unit_spec)
        out_shape.append(jax.ShapeDtypeStruct((bsz, seq, N_HEADS * V_DIM), BF16))
        args += [qp, kp, vp]
    grid_spec = pltpu.PrefetchScalarGridSpec(
        num_scalar_prefetch=1,
        grid=(db, n_steps),
        in_specs=in_specs,
        out_specs=out_specs,
        scratch_shapes=[pltpu.VMEM((pp, rows, LANES), F32), pltpu.VMEM((pp, rows, LANES), F32),
                        pltpu.VMEM((pp, rows, V_DIM), F32)],
    )
    return pl.pallas_call(
        functools.partial(_attn_sample_kernel, pp=pp, n_new=n_new, n_steps=n_steps, prompt_tiles=prompt_tiles),
        grid_spec=grid_spec,
        out_shape=out_shape,
        compiler_params=_cparams(("arbitrary", "arbitrary")),
        name="attn_sample",
    )(page_table.reshape(-1), *args)


def _route(logits):
    lane = lax.broadcasted_iota(I32, logits.shape, 1)
    lane_f = lane.astype(F32)
    big = float(LANES)
    is_grp = lane < N_GROUPS
    lg = jnp.where(is_grp, logits, -jnp.inf)
    mg = jnp.max(lg, axis=-1, keepdims=True)
    g_sel = jnp.min(jnp.where(lg == mg, lane_f, big), axis=-1, keepdims=True)
    p_grp = 1.0 / jnp.sum(jnp.exp(lg - mg), axis=-1, keepdims=True)
    lo = N_GROUPS + EXPERTS_PER_GROUP * g_sel
    le = jnp.where((lane_f >= lo) & (lane_f < lo + EXPERTS_PER_GROUP), logits, -jnp.inf)
    m1 = jnp.max(le, axis=-1, keepdims=True)
    i1 = jnp.min(jnp.where(le == m1, lane_f, big), axis=-1, keepdims=True)
    le2 = jnp.where(lane_f == i1, -jnp.inf, le)
    m2 = jnp.max(le2, axis=-1, keepdims=True)
    i2 = jnp.min(jnp.where(le2 == m2, lane_f, big), axis=-1, keepdims=True)
    e2 = jnp.exp(m2 - m1)
    w1 = p_grp / (1.0 + e2)
    w2 = p_grp * e2 / (1.0 + e2)
    wts = jnp.where(lane == 0, w1, jnp.where(lane == 1, w2, 0.0))
    eid = jnp.where(lane == 0, i1 - N_GROUPS, jnp.where(lane == 1, i2 - N_GROUPS, 0.0)).astype(I32)
    return wts, eid


def _merge_kernel(x_ref, o_ref, conv_ref, sgc_ref, sga_ref, wa_ref, wo_ref, gf_ref, wr_ref, br_ref,
                  x1_ref, h2_ref, wts_ref, eid_ref):
    attn = jnp.dot(o_ref[...], wa_ref[...], preferred_element_type=F32)
    merged = sgc_ref[...].astype(F32) * conv_ref[...] + sga_ref[...].astype(F32) * attn
    x1 = x_ref[...] + jnp.dot(merged.astype(BF16), wo_ref[...], preferred_element_type=F32)
    x1_ref[...] = x1
    inv = lax.rsqrt(jnp.mean(x1 * x1, axis=-1, keepdims=True) + RMS_EPS)
    h2 = x1 * inv * gf_ref[...]
    h2_ref[...] = h2
    logits = jnp.dot(h2.astype(BF16), wr_ref[...], preferred_element_type=F32) + br_ref[...]
    wts, eid = _route(logits)
    wts_ref[...] = wts
    eid_ref[...] = eid


def _merge(x, o, conv, sgc, sga, wa, wo, gf, wr, br, tm=256):
    t, d = x.shape
    tm = min(tm, t)
    row = lambda w: pl.BlockSpec((tm, w), lambda i: (i, 0))
    single = lambda shape: pl.BlockSpec(shape, lambda i: (0,) * len(shape), pipeline_mode=pl.Buffered(1))
    return pl.pallas_call(
        _merge_kernel,
        grid=(t // tm,),
        in_specs=[row(d), row(d), row(d), row(d), row(d), single((d, d)), single((d, d)),
                  _full((1, d)), _full((d, LANES)), _full((1, LANES))],
        out_specs=[row(d), row(d), row(LANES), row(LANES)],
        out_shape=[jax.ShapeDtypeStruct((t, d), F32), jax.ShapeDtypeStruct((t, d), F32),
                   jax.ShapeDtypeStruct((t, LANES), F32), jax.ShapeDtypeStruct((t, LANES), I32)],
        compiler_params=_cparams(("parallel",)),
        name="merge_route",
    )(x, o, conv, sgc, sga, wa, wo, gf, wr, br)


def _rank_kernel(eid_ref, rank_ref, cnt_ref, carry_ref, *, tr):
    @pl.when(pl.program_id(0) == 0)
    def _():
        carry_ref[...] = jnp.zeros(carry_ref.shape, F32)

    eid = eid_ref[...]
    lane = lax.broadcasted_iota(I32, eid.shape, 1)
    e1 = jnp.sum(jnp.where(lane == 0, eid, 0), axis=-1, keepdims=True)
    e2 = jnp.sum(jnp.where(lane == 1, eid, 0), axis=-1, keepdims=True)
    hit1 = lane == e1
    hit2 = lane == e2
    onehot = jnp.where(hit1 | hit2, 1.0, 0.0)
    r = lax.broadcasted_iota(I32, (tr, tr), 0)
    c = lax.broadcasted_iota(I32, (tr, tr), 1)
    below = jnp.where(c < r, 1.0, 0.0).astype(BF16)
    carry = carry_ref[0:1, :]
    prefix = jnp.dot(below, onehot.astype(BF16), preferred_element_type=F32) + carry
    r1 = jnp.sum(jnp.where(hit1, prefix, 0.0), axis=-1, keepdims=True)
    r2 = jnp.sum(jnp.where(hit2, prefix, 0.0), axis=-1, keepdims=True)
    rank_ref[...] = jnp.where(lane == 0, r1, jnp.where(lane == 1, r2, 0.0)).astype(I32)
    total = carry + jnp.sum(onehot, axis=0, keepdims=True)
    carry_ref[...] = jnp.broadcast_to(total, carry_ref.shape)
    cnt_ref[...] = jnp.broadcast_to(total, cnt_ref.shape)


def _rank(eid, tr):
    t = eid.shape[0]
    return pl.pallas_call(
        functools.partial(_rank_kernel, tr=tr),
        grid=(t // tr,),
        in_specs=[pl.BlockSpec((tr, LANES), lambda i: (i, 0))],
        out_specs=[pl.BlockSpec((tr, LANES), lambda i: (i, 0)), _full((8, LANES))],
        out_shape=[jax.ShapeDtypeStruct((t, LANES), I32), jax.ShapeDtypeStruct((8, LANES), F32)],
        scratch_shapes=[pltpu.VMEM((8, LANES), F32)],
        compiler_params=_cparams(("arbitrary",)),
        name="expert_rank",
    )(eid)


def _tile_slots(pos, rows):
    t = pos.shape[1]
    return pos.reshape(2, t // rows, rows).transpose(1, 0, 2).reshape(t // rows, 1, 2 * rows)


def _dispatch_kernel(pos_ref, h_ref, xs_in_ref, xs_ref, sem, *, td):
    del xs_in_ref

    def row_copy(r, k):
        p = pos_ref[0, 0, k * td + r]
        return pltpu.make_async_copy(h_ref.at[pl.ds(r, 1)], xs_ref.at[pl.ds(p, 1)], sem)

    def start(r, carry):
        row_copy(r, 0).start()
        row_copy(r, 1).start(priority=1)
        return carry

    lax.fori_loop(0, td, start, 0, unroll=DMA_UNROLL)
    for _ in range(2):
        pltpu.make_async_copy(h_ref, xs_ref.at[pl.ds(0, td)], sem).wait()


def _dispatch(h2, pos, xs0, td):
    t, d = h2.shape
    pos3 = _tile_slots(pos, td)
    return pl.pallas_call(
        functools.partial(_dispatch_kernel, td=td),
        grid=(t // td,),
        in_specs=[pl.BlockSpec((1, 1, 2 * td), lambda i: (i, 0, 0), memory_space=pltpu.SMEM),
                  pl.BlockSpec((td, d), lambda i: (i, 0)),
                  pl.BlockSpec(memory_space=pl.ANY)],
        out_specs=pl.BlockSpec(memory_space=pl.ANY),
        out_shape=jax.ShapeDtypeStruct(xs0.shape, F32),
        scratch_shapes=[pltpu.SemaphoreType.DMA(())],
        input_output_aliases={2: 0},
        compiler_params=_cparams(("arbitrary",)),
        name="moe_dispatch",
    )(pos3, h2, xs0)


def _experts_kernel(te_ref, nu_ref, x_ref, wg_ref, wu_ref, wd_ref, y_ref, wgb_ref, wub_ref, wdb_ref):
    i = pl.program_id(0)
    active = i < nu_ref[0]
    new_expert = (i == 0) | (te_ref[i] != te_ref[jnp.maximum(i - 1, 0)])

    @pl.when(active & new_expert)
    def _():
        wgb_ref[...] = wg_ref[...].astype(BF16)
        wub_ref[...] = wu_ref[...].astype(BF16)
        wdb_ref[...] = wd_ref[...].astype(BF16)

    @pl.when(active)
    def _():
        x = x_ref[...].astype(BF16)
        hg = jnp.dot(x, wgb_ref[...], preferred_element_type=F32)
        hu = jnp.dot(x, wub_ref[...], preferred_element_type=F32)
        act = hg * jax.nn.sigmoid(hg) * hu
        y_ref[...] = jnp.dot(act.astype(BF16), wdb_ref[...], preferred_element_type=F32)

    @pl.when(pl.program_id(0) >= nu_ref[0])
    def _():
        y_ref[...] = jnp.zeros(y_ref.shape, F32)


def _experts(xs, tile_expert, n_used, wg, wu, wd, tme):
    n_rows, d = xs.shape
    f = wg.shape[2]
    n_tiles = n_rows // tme
    tile = lambda i, te, nu: (jnp.minimum(i, nu[0] - 1), 0)
    grid_spec = pltpu.PrefetchScalarGridSpec(
        num_scalar_prefetch=2,
        grid=(n_tiles,),
        in_specs=[pl.BlockSpec((tme, d), tile),
                  pl.BlockSpec((None, d, f), lambda i, te, nu: (te[i], 0, 0)),
                  pl.BlockSpec((None, d, f), lambda i, te, nu: (te[i], 0, 0)),
                  pl.BlockSpec((None, f, d), lambda i, te, nu: (te[i], 0, 0))],
        out_specs=pl.BlockSpec((tme, d), lambda i, te, nu: (i, 0)),
        scratch_shapes=[pltpu.VMEM((d, f), BF16), pltpu.VMEM((d, f), BF16), pltpu.VMEM((f, d), BF16)],
    )
    return pl.pallas_call(
        _experts_kernel,
        grid_spec=grid_spec,
        out_shape=jax.ShapeDtypeStruct((n_rows, d), F32),
        compiler_params=_cparams(("arbitrary",)),
        name="moe_experts",
    )(tile_expert, n_used, xs, wg, wu, wd)


def _combine_kernel(pos_ref, x1_ref, wts_ref, gfin_ref, y_ref, o_ref, buf_ref, sem, *, tc):
    def row_copy(r, k):
        p = pos_ref[0, 0, k * tc + r]
        return pltpu.make_async_copy(y_ref.at[pl.ds(p, 1)], buf_ref.at[k, pl.ds(r, 1)], sem)

    def start(r, carry):
        row_copy(r, 0).start()
        row_copy(r, 1).start(priority=1)
        return carry

    lax.fori_loop(0, tc, start, 0, unroll=DMA_UNROLL)
    for k in range(2):
        pltpu.make_async_copy(y_ref.at[pl.ds(0, tc)], buf_ref.at[k], sem).wait()
    wts = wts_ref[...]
    x2 = x1_ref[...] + wts[:, 0:1] * buf_ref[0] + wts[:, 1:2] * buf_ref[1]
    inv = lax.rsqrt(jnp.mean(x2 * x2, axis=-1, keepdims=True) + RMS_EPS)
    o_ref[...] = x2 * inv * gfin_ref[...]


def _combine(x1, wts, pos, y, gfin, tc):
    t, d = x1.shape
    pos3 = _tile_slots(pos, tc)
    return pl.pallas_call(
        functools.partial(_combine_kernel, tc=tc),
        grid=(t // tc,),
        in_specs=[pl.BlockSpec((1, 1, 2 * tc), lambda i: (i, 0, 0), memory_space=pltpu.SMEM),
                  pl.BlockSpec((tc, d), lambda i: (i, 0)),
                  pl.BlockSpec((tc, LANES), lambda i: (i, 0)),
                  _full((1, d)),
                  pl.BlockSpec(memory_space=pl.ANY)],
        out_specs=pl.BlockSpec((tc, d), lambda i: (i, 0)),
        out_shape=jax.ShapeDtypeStruct((t, d), F32),
        scratch_shapes=[pltpu.VMEM((2, tc, d), F32), pltpu.SemaphoreType.DMA(())],
        compiler_params=_cparams(("arbitrary",)),
        name="moe_combine",
    )(pos3, x1, wts, gfin, y)


def _moe(groups, wg, wu, wd, gfin, tme, tr, td):
    sizes = [g[0].shape[0] for g in groups]
    t = sum(sizes)
    eid = jnp.concatenate([g[3] for g in groups], axis=0)
    rank, cnt = _rank(eid, min(tr, t))
    counts = cnt[0, :N_EXPERTS].astype(I32)
    padded = ((counts + tme - 1) // tme) * tme
    ends = jnp.cumsum(padded)
    offsets = ends - padded
    pos = (offsets[eid[:, :2]] + rank[:, :2]).astype(I32).T
    n_tiles = (2 * t + N_EXPERTS * (tme - 1)) // tme
    tile_start = jnp.arange(n_tiles, dtype=I32) * tme
    tile_expert = jnp.minimum(
        jnp.sum((ends[None, :] <= tile_start[:, None]).astype(I32), axis=1), N_EXPERTS - 1).astype(I32)
    n_used = (ends[-1:] // tme).astype(I32)
    starts =[sum(sizes[:i]) for i in range(len(sizes))]
    xs = jnp.zeros((n_tiles * tme, groups[0][1].shape[1]), F32)
    for (x1, h2, wts, _), s0, n in zip(groups, starts, sizes):
        xs = _dispatch(h2, pos[:, s0:s0 + n], xs, min(td, n))
    y = _experts(xs, tile_expert, n_used, wg, wu, wd, tme)
    return [_combine(x1, wts, pos[:, s0:s0 + n], y, gfin, min(td, n))
            for (x1, h2, wts, _), s0, n in zip(groups, starts, sizes)]


def kernel(x_prompt, x_sample, cache_k, cache_v, state_conv, page_table, norm_mix_g, w_in, conv_dw_w, conv_dw_b, conv_ln_g, conv_ln_b, conv_w_out, lambda_q1, lambda_k1, lambda_q2, lambda_k2, head_norm_g, attn_w_out, w_out, norm_ffn_g, router_group_w, router_group_b, router_expert_w, router_expert_b, expert_w_gate, expert_w_up, expert_w_down, final_norm_g):
    bsz, seq, d = x_prompt.shape
    db, ds, _ = x_sample.shape
    l = LAYER
    qkw = 2 * N_HEADS * HEAD_DIM
    vw = N_HEADS * V_DIM

    win = w_in[l]
    bounds = [0, D_CONV, 2 * D_CONV, 2 * D_CONV + qkw, 2 * D_CONV + 2 * qkw, 2 * D_CONV + 2 * qkw + vw,
              2 * D_CONV + 2 * qkw + vw + d, 2 * D_CONV + 2 * qkw + vw + 2 * d]
    w_all = jnp.concatenate([win[:, bounds[1]:bounds[2]], win[:, bounds[0]:bounds[1]], win[:, bounds[2]:]],
                            axis=1).astype(BF16)
    w_pw = conv_w_out[l].astype(BF16)
    w_ao = attn_w_out[l].astype(BF16)
    w_o = w_out[l].astype(BF16)
    w_r = jnp.zeros((d, LANES), F32).at[:, :N_GROUPS].set(router_group_w[l])
    w_r = w_r.at[:, N_GROUPS:N_GROUPS + N_EXPERTS].set(router_expert_w[l]).astype(BF16)
    b_r = jnp.zeros((1, LANES), F32).at[0, :N_GROUPS].set(router_group_b[l])
    b_r = b_r.at[0, N_GROUPS:N_GROUPS + N_EXPERTS].set(router_expert_b[l])
    wg, wu, wd = expert_w_gate[l], expert_w_up[l], expert_w_down[l]
    row = lambda a: a.reshape(1, -1)
    g_mix, g_ffn, g_fin = row(norm_mix_g[l]), row(norm_ffn_g[l]), row(final_norm_g)
    dww, dwb = conv_dw_w[l], row(conv_dw_b[l])
    lng, lnb = row(conv_ln_g[l]), row(conv_ln_b[l])
    lams = [row(lambda_q1[l]), row(lambda_k1[l]), row(lambda_q2[l]), row(lambda_k2[l])]
    g_head = row(head_norm_g[l])

    def project(x2d, tm):
        return _project(x2d, g_mix, w_all, tm)

    def merge(x2d, o, conv, sgc, sga):
        return _merge(x2d, o, conv, sgc, sga, w_ao, w_o, g_ffn, w_r, b_r)

    tp = bsz * seq
    xp = x_prompt.reshape(tp, d)
    u_p, q_p, k_p, kb_p, v_p, vb_p, sgc_p, sga_p = project(xp, 1024)
    conv_p = _conv_prompt(u_p.reshape(bsz, seq, D_CONV), dww, dwb, lng, lnb, w_pw).reshape(tp, d)
    qkv_p = (q_p.reshape(bsz, seq, qkw), kb_p.reshape(bsz, seq, qkw), vb_p.reshape(bsz, seq, vw))

    tsamp = db * ds
    xs_ = x_sample.reshape(tsamp, d)
    u_s, q_s, k_s, kb_s, v_s, vb_s, sgc_s, sga_s = project(xs_, tsamp)
    u_ext = jnp.concatenate([state_conv[l], u_s.reshape(db, ds, D_CONV)], axis=1)
    conv_s = _conv_sample(jnp.swapaxes(u_ext, 0, 1), dww, dwb, lng, lnb, w_pw)
    conv_s = jnp.swapaxes(conv_s, 0, 1).reshape(tsamp, d)
    q5 = q_s.reshape(db, ds, N_HEADS, 2, HEAD_DIM).transpose(0, 2, 3, 1, 4)
    zero = jnp.zeros_like(q5[:, :, 0])
    qbd = jnp.stack([jnp.concatenate([q5[:, :, 0], zero], axis=-1),
                     jnp.concatenate([zero, q5[:, :, 1]], axis=-1)], axis=2).reshape(db, N_HEADS * 2 * ds, V_DIM)
    pad_rows = NEW_PAD - ds * N_HEADS
    kn = jnp.pad(kb_s.reshape(db, ds * N_HEADS, V_DIM), ((0, 0), (0, pad_rows), (0, 0)))
    vn = jnp.pad(vb_s.reshape(db, ds * N_HEADS, V_DIM), ((0, 0), (0, pad_rows), (0, 0)))
    n_pool = cache_k.shape[1]
    ck = cache_k.reshape(cache_k.shape[0] * n_pool, PAGE_SIZE * N_HEADS, V_DIM)
    cv = cache_v.reshape(cache_v.shape[0] * n_pool, PAGE_SIZE * N_HEADS, V_DIM)
    tq = min(PROMPT_TQ, seq)
    pt = page_table + l * n_pool
    if db * (page_table.shape[1] // ATTN_PAGES) >= bsz * N_HEADS * (seq // tq):
        o_s, o_p = _attn_sample(qbd, kn, vn, ck, cv, pt, lams, g_head, prompt=(*qkv_p, tq), pp=ATTN_PAGES)
    else:
        (o_s,) = _attn_sample(qbd, kn, vn, ck, cv, pt, lams, g_head, pp=ATTN_PAGES)
        o_p = _attn_prompt(*qkv_p, lams, g_head, tq=tq)
    group_p = merge(xp, o_p.reshape(tp, vw), conv_p, sgc_p, sga_p)
    group_s = merge(xs_, o_s.reshape(tsamp, vw), conv_s, sgc_s, sga_s)

    t_all = tp + tsamp
    tr = next(c for c in (512, 256, 128) if t_all % c == 0)
    y_p, y_s = _moe([group_p, group_s], wg, wu, wd, g_fin, MOE_TILE, tr, 256)
    y_p = y_p.reshape(bsz, seq, d)
    y_s = y_s.reshape(db, ds, d)

    k_prompt = k_p.reshape(1, bsz, seq, N_HEADS, 2 * HEAD_DIM)
    v_prompt = v_p.reshape(1, bsz, seq, N_HEADS, V_DIM)
    conv_prompt = u_p.reshape(bsz, seq, D_CONV)[:, seq - (CONV_W - 1):][None]
    k_sample = k_s.reshape(1, db, ds, N_HEADS, 2 * HEAD_DIM)
    v_sample = v_s.reshape(1, db, ds, N_HEADS, V_DIM)
    conv_sample = u_ext[:, ds:][None]
    return (y_p, y_s, k_prompt, v_prompt, conv_prompt, k_sample, v_sample, conv_sample)
```

```python
import functools
import math

import jax
import jax.numpy as jnp
from jax import lax
from jax.experimental import pallas as pl
from jax.experimental.pallas import tpu as pltpu

F32 = jnp.float32
BF16 = jnp.bfloat16
I32 = jnp.int32

D_MODEL = 2048
HEAD_DIM = 128
N_HEADS = D_MODEL // (2 * HEAD_DIM)
V_DIM = 2 * HEAD_DIM
D_CONV = D_MODEL // 2
CONV_W = 31
N_GROUPS = 4
EXPERTS_PER_GROUP = 8
N_EXPERTS = N_GROUPS * EXPERTS_PER_GROUP
D_EXPERT = D_MODEL // 4
PAGE_SIZE = 128
RMS_EPS = 1e-6
LN_EPS = 1e-5
ALIBI_MAX_EXP = 8.0
LAYER = 0
LAM_INIT = 0.8 - 0.6 * math.exp(-0.3 * LAYER)
LOG2E = math.log2(math.e)

LANES = 128
SUBLANES = 8
NEG_BIG = -1e30
PROMPT_TQ = 512
ATTN_PAGES = 8
MOE_TILE = 512
VMEM_LIMIT = 56 * 1024 * 1024


def _cparams(sem):
    return pltpu.CompilerParams(dimension_semantics=sem, vmem_limit_bytes=VMEM_LIMIT)


def _full(shape):
    return pl.BlockSpec(shape, lambda *_: (0,) * len(shape))


PROJ_SEGMENTS = (("glu_b", D_CONV), ("glu_a", D_CONV), ("q", D_MODEL), ("k", D_MODEL), ("v", D_MODEL),
                 ("g_conv", D_MODEL), ("g_attn", D_MODEL))


def _proj_kernel(x_ref, g_ref, w_ref, u_ref, q_ref, k_ref, kb_ref, v_ref, vb_ref, gc_ref, ga_ref, h_ref, sig_ref,
                 *, tn):
    j = pl.program_id(1)

    @pl.when(j == 0)
    def _():
        x = x_ref[...]
        inv = lax.rsqrt(jnp.mean(x * x, axis=-1, keepdims=True) + RMS_EPS)
        h_ref[...] = (x * inv * g_ref[...]).astype(BF16)

    z = jnp.dot(h_ref[...], w_ref[...], preferred_element_type=F32)

    def glu_b(jj):
        sig_ref[jj] = jax.nn.sigmoid(z)

    def glu_a(jj):
        u_ref[...] = z * sig_ref[jj]

    def q_seg(jj):
        q_ref[...] = (z * (LOG2E * HEAD_DIM ** -0.5)).astype(q_ref.dtype)

    def k_seg(jj):
        k_ref[...] = z
        kb_ref[...] = z.astype(kb_ref.dtype)

    def v_seg(jj):
        v_ref[...] = z
        vb_ref[...] = z.astype(vb_ref.dtype)

    def gc_seg(jj):
        gc_ref[...] = jax.nn.sigmoid(z).astype(gc_ref.dtype)

    def ga_seg(jj):
        ga_ref[...] = jax.nn.sigmoid(z).astype(ga_ref.dtype)

    start = 0
    for (_, width), fn in zip(PROJ_SEGMENTS, (glu_b, glu_a, q_seg, k_seg, v_seg, gc_seg, ga_seg)):
        blocks = width // tn
        if fn in (glu_b, glu_a):
            for jj in range(blocks):
                pl.when(j == start + jj)(functools.partial(fn, jj))
        else:
            pl.when((j >= start) & (j < start + blocks))(functools.partial(fn, 0))
        start += blocks


def _project(x, g, w_all, tm, tn=512):
    t, d = x.shape
    n = w_all.shape[1]
    starts, s0 = {}, 0
    for name, width in PROJ_SEGMENTS:
        starts[name] = (s0 // tn, width // tn)
        s0 += width

    def seg_spec(name):
        first, blocks = starts[name]
        return pl.BlockSpec((tm, tn), lambda i, j: (i, jnp.clip(j - first, 0, blocks - 1)))

    outs = [("glu_a", D_CONV, F32), ("q", D_MODEL, BF16), ("k", D_MODEL, F32), ("k", D_MODEL, BF16),
            ("v", D_MODEL, F32), ("v", D_MODEL, BF16), ("g_conv", D_MODEL, BF16), ("g_attn", D_MODEL, BF16)]
    return pl.pallas_call(
        functools.partial(_proj_kernel, tn=tn),
        grid=(t // tm, n // tn),
        in_specs=[pl.BlockSpec((tm, d), lambda i, j: (i, 0)), _full((1, d)),
                  pl.BlockSpec((d, tn), lambda i, j: (0, j))],
        out_specs=[seg_spec(name) for name, _, _ in outs],
        out_shape=[jax.ShapeDtypeStruct((t, width), dt) for _, width, dt in outs],
        scratch_shapes=[pltpu.VMEM((tm, d), BF16), pltpu.VMEM((D_CONV // tn, tm, tn), F32)],
        compiler_params=_cparams(("arbitrary", "arbitrary")),
        name="proj",
    )(x, g, w_all)


def _ln_swish_project(z, lng_ref, lnb_ref, wpw_ref):
    mu = jnp.mean(z, axis=-1, keepdims=True)
    zc = z - mu
    var = jnp.mean(zc * zc, axis=-1, keepdims=True)
    y = zc * lax.rsqrt(var + LN_EPS) * lng_ref[...] + lnb_ref[...]
    y = y * jax.nn.sigmoid(y)
    return jnp.dot(y.astype(BF16), wpw_ref[...], preferred_element_type=F32)


HALO = 32
CONV_ROWS = 128


def _conv_prompt_kernel(u_ref, halo_ref, dww_ref, dwb_ref, lng_ref, lnb_ref, wpw_ref, o_ref, ext_ref, z_ref, sh_ref,
                        *, ts):
    first = pl.program_id(1) == 0
    ext_ref[0:HALO, :] = jnp.where(first, 0.0, halo_ref[...])
    ext_ref[HALO:, :] = u_ref[...]
    base = HALO - (CONV_W - 1)
    for cb in range(D_CONV // LANES):
        cs = slice(cb * LANES, (cb + 1) * LANES)
        for shift in range(SUBLANES):
            span = ts + HALO - (SUBLANES if shift else 0)
            sh_ref[shift, 0:span, :] = ext_ref[shift:shift + span, cs]
        for r0 in range(0, ts, CONV_ROWS):
            acc = jnp.broadcast_to(dwb_ref[:, cs], (CONV_ROWS, LANES))
            for j in range(CONV_W):
                shift = (base + j) % SUBLANES
                a = base + j - shift + r0
                acc = acc + dww_ref[j:j + 1, cs] * sh_ref[shift, a:a + CONV_ROWS, :]
            z_ref[r0:r0 + CONV_ROWS, cs] = acc
    o_ref[...] = _ln_swish_project(z_ref[...], lng_ref, lnb_ref, wpw_ref)


def _conv_prompt(u, dww, dwb, lng, lnb, wpw, ts=256):
    b, s, c = u.shape
    d = wpw.shape[1]
    r = ts // HALO
    return pl.pallas_call(
        functools.partial(_conv_prompt_kernel, ts=ts),
        grid=(b, s // ts),
        in_specs=[
            pl.BlockSpec((None, ts, c), lambda bi, i: (bi, i, 0)),
            pl.BlockSpec((None, HALO, c), lambda bi, i: (bi, jnp.maximum(i * r - 1, 0), 0)),
            _full((CONV_W, c)), _full((1, c)), _full((1, c)), _full((1, c)), _full((c, d)),
        ],
        out_specs=pl.BlockSpec((None, ts, d), lambda bi, i: (bi, i, 0)),
        out_shape=jax.ShapeDtypeStruct((b, s, d), F32),
        scratch_shapes=[pltpu.VMEM((ts + HALO, c), F32), pltpu.VMEM((ts, c), F32),
                        pltpu.VMEM((SUBLANES, ts + HALO, LANES), F32)],
        compiler_params=_cparams(("parallel", "arbitrary")),
        name="conv_prompt",
    )(u, u, dww, dwb, lng, lnb, wpw)


def _conv_sample_kernel(ext_ref, dww_ref, dwb_ref, lng_ref, lnb_ref, wpw_ref, o_ref, *, n_new, bt):
    zs = []
    for t in range(n_new):
        acc = jnp.broadcast_to(dwb_ref[...], (bt, D_CONV))
        for j in range(CONV_W):
            acc = acc + dww_ref[j:j + 1, :] * ext_ref[t + j]
        zs.append(acc)
    out = _ln_swish_project(jnp.concatenate(zs, axis=0), lng_ref, lnb_ref, wpw_ref)
    for t in range(n_new):
        o_ref[t] = out[t * bt:(t + 1) * bt]


def _conv_sample(ext_t, dww, dwb, lng, lnb, wpw, bt=32):
    rows, b, c = ext_t.shape
    n_new = rows - (CONV_W - 1)
    d = wpw.shape[1]
    return pl.pallas_call(
        functools.partial(_conv_sample_kernel, n_new=n_new, bt=bt),
        grid=(b // bt,),
        in_specs=[
            pl.BlockSpec((rows, bt, c), lambda i: (0, i, 0)),
            _full((CONV_W, c)), _full((1, c)), _full((1, c)), _full((1, c)), _full((c, d)),
        ],
        out_specs=pl.BlockSpec((n_new, bt, d), lambda i: (0, i, 0)),
        out_shape=jax.ShapeDtypeStruct((n_new, b, d), F32),
        compiler_params=_cparams(("parallel",)),
        name="conv_sample",
    )(ext_t, dww, dwb, lng, lnb, wpw)


def _lambda_value(lq1_ref, lk1_ref, lq2_ref, lk2_ref):
    a = jnp.sum(lq1_ref[...] * lk1_ref[...], axis=-1, keepdims=True)
    b = jnp.sum(lq2_ref[...] * lk2_ref[...], axis=-1, keepdims=True)
    return jnp.exp(a) - jnp.exp(b) + LAM_INIT


def _head_norm(o, g_ref):
    inv = lax.rsqrt(jnp.mean(o * o, axis=-1, keepdims=True) + RMS_EPS)
    return o * inv * g_ref[...] * (1.0 - LAM_INIT)


def _alibi_slope(h, shape):
    return jnp.exp2(jnp.full(shape, -(ALIBI_MAX_EXP / N_HEADS), F32) * (h + 1).astype(F32))


def _nt_dot(a, b):
    return lax.dot_general(a, b, (((1,), (1,)), ((), ())), preferred_element_type=F32)


def _prompt_query_tile(i, h, q_ref, k_ref, v_ref, lam_refs, g_ref, o_ref, *, tq):
    lo, hi = i * tq, (i + 1) * tq
    kpos = lax.broadcasted_iota(I32, (1, hi), 1).astype(F32)
    bias = _alibi_slope(h, (1, hi)) * LOG2E * kpos
    lam = _lambda_value(*lam_refs)
    visible = lax.broadcasted_iota(I32, (tq, tq), 1) <= lax.broadcasted_iota(I32, (tq, tq), 0)
    outs = []
    for c in range(2):
        cs = slice(c * HEAD_DIM, (c + 1) * HEAD_DIM)
        q = q_ref[lo:hi, cs]
        s_d = jnp.where(visible, _nt_dot(q, k_ref[lo:hi, cs]) + bias[:, lo:hi], -jnp.inf)
        m = jnp.max(s_d, axis=-1, keepdims=True)
        if i > 0:
            s_o = _nt_dot(q, k_ref[0:lo, cs]) + bias[:, 0:lo]
            m = jnp.maximum(m, jnp.max(s_o, axis=-1, keepdims=True))
        p_d = jnp.exp2(s_d - m)
        l = jnp.sum(p_d, axis=-1, keepdims=True)
        pv = jnp.dot(p_d.astype(BF16), v_ref[lo:hi, :], preferred_element_type=F32)
        if i > 0:
            p_o = jnp.exp2(s_o - m)
            l = l + jnp.sum(p_o, axis=-1, keepdims=True)
            pv = pv + jnp.dot(p_o.astype(BF16), v_ref[0:lo, :], preferred_element_type=F32)
        outs.append(pv / l)
    o_ref[lo:hi, :] = _head_norm(outs[0] - lam * outs[1], g_ref).astype(o_ref.dtype)


def _attn_prompt_kernel(q_ref, k_ref, v_ref, lq1_ref, lk1_ref, lq2_ref, lk2_ref, g_ref, o_ref, *, tq, seq):
    h = pl.program_id(1)
    for i in range(seq // tq):
        _prompt_query_tile(i, h, q_ref, k_ref, v_ref, (lq1_ref, lk1_ref, lq2_ref, lk2_ref), g_ref, o_ref, tq=tq)


def _attn_prompt(q, k, v, lams, g, tq=512):
    b, s, _ = q.shape
    w = 2 * HEAD_DIM
    spec = pl.BlockSpec((None, s, w), lambda bi, h: (bi, 0, h))
    return pl.pallas_call(
        functools.partial(_attn_prompt_kernel, tq=min(tq, s), seq=s),
        grid=(b, N_HEADS),
        in_specs=[spec, spec, spec] + [_full((1, HEAD_DIM))] * 4 + [_full((1, V_DIM))],
        out_specs=pl.BlockSpec((None, s, V_DIM), lambda bi, h: (bi, 0, h)),
        out_shape=jax.ShapeDtypeStruct((b, s, N_HEADS * V_DIM), BF16),
        compiler_params=_cparams(("parallel", "parallel")),
        name="attn_prompt",
    )(q, k, v, *lams, g)


NEW_PAD = 128


def _attn_sample_kernel(pt_ref, q_ref, kn_ref, vn_ref, bias_ref, off_ref, biasn_ref, *refs, pp, n_new, n_steps,
                        prompt_tiles):
    k_refs = refs[:pp]
    v_refs = refs[pp:2 * pp]
    lq1_ref, lk1_ref, lq2_ref, lk2_ref, g_ref = refs[2 * pp:2 * pp + 5]
    rest = refs[2 * pp + 5:]
    if prompt_tiles is not None:
        qp_ref, kp_ref, vp_ref, o_ref, op_ref, m_ref, l_ref, acc_ref = rest
    else:
        o_ref, m_ref, l_ref, acc_ref = rest
    step = pl.program_id(1)
    q = q_ref[...]

    @pl.when(step == 0)
    def _():
        m_ref[...] = jnp.full(m_ref.shape, NEG_BIG, F32)
        l_ref[...] = jnp.zeros(l_ref.shape, F32)
        acc_ref[...] = jnp.zeros(acc_ref.shape, F32)

    def update(c, s, off, v):
        m_prev = m_ref[c][:, :1]
        m_new = jnp.maximum(m_prev, jnp.max(s, axis=-1, keepdims=True) + off)
        alpha = jnp.exp2(m_prev - m_new)
        p = jnp.exp2(s - (m_new - off))
        l_new = alpha * l_ref[c][:, :1] + jnp.sum(p, axis=-1, keepdims=True)
        acc_ref[c] = alpha * acc_ref[c] + jnp.dot(p.astype(BF16), v, preferred_element_type=F32)
        m_ref[c] = jnp.broadcast_to(m_new, m_ref.shape[1:])
        l_ref[c] = jnp.broadcast_to(l_new, l_ref.shape[1:])

    def stream_pages():
        scores = [_nt_dot(q, k_refs[j][...].astype(BF16)) + bias_ref[...] for j in range(pp)]
        offs = [off_ref[:, :1] * (step * pp + j).astype(F32) for j in range(pp)]
        m_prevs = [m_ref[j][:, :1] for j in range(pp)]
        m_news = [jnp.maximum(m_prevs[j], jnp.max(scores[j], axis=-1, keepdims=True) + offs[j]) for j in range(pp)]
        ps = [jnp.exp2(scores[j] - (m_news[j] - offs[j])) for j in range(pp)]
        pvs = [jnp.dot(ps[j].astype(BF16), v_refs[j][...].astype(BF16), preferred_element_type=F32)
               for j in range(pp)]
        for j in range(pp):
            alpha = jnp.exp2(m_prevs[j] - m_news[j])
            l_new = alpha * l_ref[j][:, :1] + jnp.sum(ps[j], axis=-1, keepdims=True)
            acc_ref[j] = alpha * acc_ref[j] + pvs[j]
            m_ref[j] = jnp.broadcast_to(m_news[j], m_ref.shape[1:])
            l_ref[j] = jnp.broadcast_to(l_new, l_ref.shape[1:])

    if prompt_tiles is None:
        stream_pages()
    else:
        per_unit, tq, total = prompt_tiles
        flat = pl.program_id(0) * n_steps + step
        head = (flat // per_unit) % N_HEADS
        for i in range(per_unit):
            @pl.when((flat < total) & (flat % per_unit == i))
            def _(i=i):
                _prompt_query_tile(i, head, qp_ref, kp_ref, vp_ref, (lq1_ref, lk1_ref, lq2_ref, lk2_ref), g_ref,
                                   op_ref, tq=tq)
                stream_pages()

        pl.when(flat >= total)(stream_pages)

    @pl.when(step == n_steps - 1)
    def _():
        update(0, _nt_dot(q, kn_ref[...]) + biasn_ref[...], 0.0, vn_ref[...])
        m_all = m_ref[0][:, :1]
        for c in range(1, pp):
            m_all = jnp.maximum(m_all, m_ref[c][:, :1])
        l_all = jnp.zeros_like(m_all)
        acc_all = jnp.zeros(acc_ref.shape[1:], F32)
        for c in range(pp):
            w = jnp.exp2(m_ref[c][:, :1] - m_all)
            l_all = l_all + w * l_ref[c][:, :1]
            acc_all = acc_all + w * acc_ref[c]
        lam = _lambda_value(lq1_ref, lk1_ref, lq2_ref, lk2_ref)
        a = acc_all / l_all
        for h in range(N_HEADS):
            r0 = h * 2 * n_new
            o = a[r0:r0 + n_new] - lam * a[r0 + n_new:r0 + 2 * n_new]
            o_ref[:, h * V_DIM:(h + 1) * V_DIM] = _head_norm(o, g_ref).astype(o_ref.dtype)


def _attn_sample(q_all, kn, vn, cache_k, cache_v, page_table, lams, g, prompt=None, pp=8):
    db, n_pages = page_table.shape
    rows = q_all.shape[1]
    n_new = rows // (2 * N_HEADS)
    n_steps = n_pages // pp
    pr = PAGE_SIZE * N_HEADS

    slopes = 2.0 ** (-(ALIBI_MAX_EXP / N_HEADS) * (jnp.arange(rows, dtype=F32) // (2 * n_new) + 1.0))[:, None]
    row_head = (jnp.arange(rows, dtype=I32) // (2 * n_new))[:, None]
    row_q = (jnp.arange(rows, dtype=I32) % n_new)[:, None]
    col = jnp.arange(pr, dtype=I32)[None, :]
    bias = jnp.where(col % N_HEADS == row_head, slopes * LOG2E * (col // N_HEADS).astype(F32), -jnp.inf)
    off = jnp.broadcast_to(slopes * LOG2E * PAGE_SIZE, (rows, LANES))
    coln = jnp.arange(NEW_PAD, dtype=I32)[None, :]
    new_ok = (coln % N_HEADS == row_head) & (coln // N_HEADS <= row_q)
    biasn = jnp.where(new_ok, slopes * LOG2E * (n_pages * PAGE_SIZE + coln // N_HEADS).astype(F32), -jnp.inf)

    def page_spec(j):
        return pl.BlockSpec((None, pr, V_DIM), lambda b, s, pt: (pt[b * n_pages + s * pp + j], 0, 0))

    const = lambda shape: pl.BlockSpec(shape, lambda b, s, pt: (0,) * len(shape))
    per_seq = lambda r: pl.BlockSpec((None, r, V_DIM), lambda b, s, pt: (b, 0, 0))
    in_specs = ([per_seq(rows), per_seq(NEW_PAD), per_seq(NEW_PAD),
                 const((rows, pr)), const((rows, LANES)), const((rows, NEW_PAD))]
                + [page_spec(j) for j in range(pp)] + [page_spec(j) for j in range(pp)]
                + [const((1, HEAD_DIM))] * 4 + [const((1, V_DIM))])
    out_specs = [pl.BlockSpec((None, n_new, N_HEADS * V_DIM), lambda b, s, pt: (b, 0, 0))]
    out_shape = [jax.ShapeDtypeStruct((db, n_new, N_HEADS * V_DIM), BF16)]
    args = [q_all, kn, vn, bias, off, biasn, *([cache_k] * pp), *([cache_v] * pp), *lams, g]
    prompt_tiles = None
    if prompt is not None:
        qp, kp, vp, tq = prompt
        bsz, seq, _ = qp.shape
        per_unit = seq // tq
        units = bsz * N_HEADS
        prompt_tiles = (per_unit, tq, units * per_unit)
        assert db * n_steps >= units * per_unit, "not enough cache steps to carry the prompt tiles"

        def unit_map(b, s, pt):
            unit = jnp.minimum((b * n_steps + s) // per_unit, units - 1)
            return (unit // N_HEADS, 0, unit % N_HEADS)

        unit_spec = pl.BlockSpec((None, seq, V_DIM), unit_map)
        in_specs += [unit_spec] * 3
        out_specs.append(unit_spec)
        out_shape.append(jax.ShapeDtypeStruct((bsz, seq, N_HEADS * V_DIM), BF16))
        args += [qp, kp, vp]
    grid_spec = pltpu.PrefetchScalarGridSpec(
        num_scalar_prefetch=1,
        grid=(db, n_steps),
        in_specs=in_specs,
        out_specs=out_specs,
        scratch_shapes=[pltpu.VMEM((pp, rows, LANES), F32), pltpu.VMEM((pp, rows, LANES), F32),
                        pltpu.VMEM((pp, rows, V_DIM), F32)],
    )
    return pl.pallas_call(
        functools.partial(_attn_sample_kernel, pp=pp, n_new=n_new, n_steps=n_steps, prompt_tiles=prompt_tiles),
        grid_spec=grid_spec,
        out_shape=out_shape,
        compiler_params=_cparams(("arbitrary", "arbitrary")),
        name="attn_sample",
    )(page_table.reshape(-1), *args)


def _route(logits):
    lane = lax.broadcasted_iota(I32, logits.shape, 1)
    lane_f = lane.astype(F32)
    big = float(LANES)
    is_grp = lane < N_GROUPS
    lg = jnp.where(is_grp, logits, -jnp.inf)
    mg = jnp.max(lg, axis=-1, keepdims=True)
    g_sel = jnp.min(jnp.where(lg == mg, lane_f, big), axis=-1, keepdims=True)
    p_grp = 1.0 / jnp.sum(jnp.exp(lg - mg), axis=-1, keepdims=True)
    lo = N_GROUPS + EXPERTS_PER_GROUP * g_sel
    le = jnp.where((lane_f >= lo) & (lane_f < lo + EXPERTS_PER_GROUP), logits, -jnp.inf)
    m1 = jnp.max(le, axis=-1, keepdims=True)
    i1 = jnp.min(jnp.where(le == m1, lane_f, big), axis=-1, keepdims=True)
    le2 = jnp.where(lane_f == i1, -jnp.inf, le)
    m2 = jnp.max(le2, axis=-1, keepdims=True)
    i2 = jnp.min(jnp.where(le2 == m2, lane_f, big), axis=-1, keepdims=True)
    e2 = jnp.exp(m2 - m1)
    w1 = p_grp / (1.0 + e2)
    w2 = p_grp * e2 / (1.0 + e2)
    wts = jnp.where(lane == 0, w1, jnp.where(lane == 1, w2, 0.0))
    eid = jnp.where(lane == 0, i1 - N_GROUPS, jnp.where(lane == 1, i2 - N_GROUPS, 0.0)).astype(I32)
    return wts, eid


def _merge_kernel(x_ref, o_ref, conv_ref, sgc_ref, sga_ref, wa_ref, wo_ref, gf_ref, wr_ref, br_ref,
                  x1_ref, h2_ref, wts_ref, eid_ref):
    attn = jnp.dot(o_ref[...], wa_ref[...], preferred_element_type=F32)
    merged = sgc_ref[...].astype(F32) * conv_ref[...] + sga_ref[...].astype(F32) * attn
    x1 = x_ref[...] + jnp.dot(merged.astype(BF16), wo_ref[...], preferred_element_type=F32)
    x1_ref[...] = x1
    inv = lax.rsqrt(jnp.mean(x1 * x1, axis=-1, keepdims=True) + RMS_EPS)
    h2 = x1 * inv * gf_ref[...]
    h2_ref[...] = h2
    logits = jnp.dot(h2.astype(BF16), wr_ref[...], preferred_element_type=F32) + br_ref[...]
    wts, eid = _route(logits)
    wts_ref[...] = wts
    eid_ref[...] = eid


def _merge(x, o, conv, sgc, sga, wa, wo, gf, wr, br, tm=256):
    t, d = x.shape
    tm = min(tm, t)
    row = lambda w: pl.BlockSpec((tm, w), lambda i: (i, 0))
    single = lambda shape: pl.BlockSpec(shape, lambda i: (0,) * len(shape), pipeline_mode=pl.Buffered(1))
    return pl.pallas_call(
        _merge_kernel,
        grid=(t // tm,),
        in_specs=[row(d), row(d), row(d), row(d), row(d), single((d, d)), single((d, d)),
                  _full((1, d)), _full((d, LANES)), _full((1, LANES))],
        out_specs=[row(d), row(d), row(LANES), row(LANES)],
        out_shape=[jax.ShapeDtypeStruct((t, d), F32), jax.ShapeDtypeStruct((t, d), F32),
                   jax.ShapeDtypeStruct((t, LANES), F32), jax.ShapeDtypeStruct((t, LANES), I32)],
        compiler_params=_cparams(("parallel",)),
        name="merge_route",
    )(x, o, conv, sgc, sga, wa, wo, gf, wr, br)


def _rank_kernel(eid_ref, rank_ref, cnt_ref, carry_ref, *, tr):
    @pl.when(pl.program_id(0) == 0)
    def _():
        carry_ref[...] = jnp.zeros(carry_ref.shape, F32)

    eid = eid_ref[...]
    lane = lax.broadcasted_iota(I32, eid.shape, 1)
    e1 = jnp.sum(jnp.where(lane == 0, eid, 0), axis=-1, keepdims=True)
    e2 = jnp.sum(jnp.where(lane == 1, eid, 0), axis=-1, keepdims=True)
    hit1 = lane == e1
    hit2 = lane == e2
    onehot = jnp.where(hit1 | hit2, 1.0, 0.0)
    r = lax.broadcasted_iota(I32, (tr, tr), 0)
    c = lax.broadcasted_iota(I32, (tr, tr), 1)
    below = jnp.where(c < r, 1.0, 0.0).astype(BF16)
    carry = carry_ref[0:1, :]
    prefix = jnp.dot(below, onehot.astype(BF16), preferred_element_type=F32) + carry
    r1 = jnp.sum(jnp.where(hit1, prefix, 0.0), axis=-1, keepdims=True)
    r2 = jnp.sum(jnp.where(hit2, prefix, 0.0), axis=-1, keepdims=True)
    rank_ref[...] = jnp.where(lane == 0, r1, jnp.where(lane == 1, r2, 0.0)).astype(I32)
    total = carry + jnp.sum(onehot, axis=0, keepdims=True)
    carry_ref[...] = jnp.broadcast_to(total, carry_ref.shape)
    cnt_ref[...] = jnp.broadcast_to(total, cnt_ref.shape)


def _rank(eid, tr):
    t = eid.shape[0]
    return pl.pallas_call(
        functools.partial(_rank_kernel, tr=tr),
        grid=(t // tr,),
        in_specs=[pl.BlockSpec((tr, LANES), lambda i: (i, 0))],
        out_specs=[pl.BlockSpec((tr, LANES), lambda i: (i, 0)), _full((8, LANES))],
        out_shape=[jax.ShapeDtypeStruct((t, LANES), I32), jax.ShapeDtypeStruct((8, LANES), F32)],
        scratch_shapes=[pltpu.VMEM((8, LANES), F32)],
        compiler_params=_cparams(("arbitrary",)),
        name="expert_rank",
    )(eid)


def _tile_slots(pos, rows):
    t = pos.shape[1]
    return pos.reshape(2, t // rows, rows).transpose(1, 0, 2).reshape(t // rows, 1, 2 * rows)


def _dispatch_kernel(pos_ref, h_ref, xs_in_ref, xs_ref, sem, *, td):
    del xs_in_ref

    def row_copy(r, k):
        p = pos_ref[0, 0, k * td + r]
        return pltpu.make_async_copy(h_ref.at[pl.ds(r, 1)], xs_ref.at[pl.ds(p, 1)], sem)

    def start(r, carry):
        row_copy(r, 0).start()
        row_copy(r, 1).start(priority=1)
        return carry

    for r in range(td):
        start(r, 0)
    for _ in range(2):
        pltpu.make_async_copy(h_ref, xs_ref.at[pl.ds(0, td)], sem).wait()


def _dispatch(h2, pos, xs0, td):
    t, d = h2.shape
    pos3 = _tile_slots(pos, td)
    return pl.pallas_call(
        functools.partial(_dispatch_kernel, td=td),
        grid=(t // td,),
        in_specs=[pl.BlockSpec((1, 1, 2 * td), lambda i: (i, 0, 0), memory_space=pltpu.SMEM),
                  pl.BlockSpec((td, d), lambda i: (i, 0)),
                  pl.BlockSpec(memory_space=pl.ANY)],
        out_specs=pl.BlockSpec(memory_space=pl.ANY),
        out_shape=jax.ShapeDtypeStruct(xs0.shape, F32),
        scratch_shapes=[pltpu.SemaphoreType.DMA(())],
        input_output_aliases={2: 0},
        compiler_params=_cparams(("arbitrary",)),
        name="moe_dispatch",
    )(pos3, h2, xs0)


def _experts_kernel(te_ref, nu_ref, x_ref, wg_ref, wu_ref, wd_ref, y_ref, wgb_ref, wub_ref, wdb_ref):
    i = pl.program_id(0)
    active = i < nu_ref[0]
    new_expert = (i == 0) | (te_ref[i] != te_ref[jnp.maximum(i - 1, 0)])

    @pl.when(active & new_expert)
    def _():
        wgb_ref[...] = wg_ref[...].astype(BF16)
        wub_ref[...] = wu_ref[...].astype(BF16)
        wdb_ref[...] = wd_ref[...].astype(BF16)

    @pl.when(active)
    def _():
        x = x_ref[...].astype(BF16)
        hg = jnp.dot(x, wgb_ref[...], preferred_element_type=F32)
        hu = jnp.dot(x, wub_ref[...], preferred_element_type=F32)
        act = hg * jax.nn.sigmoid(hg) * hu
        y_ref[...] = jnp.dot(act.astype(BF16), wdb_ref[...], preferred_element_type=F32)

    @pl.when(pl.program_id(0) >= nu_ref[0])
    def _():
        y_ref[...] = jnp.zeros(y_ref.shape, F32)


def _experts(xs, tile_expert, n_used, wg, wu, wd, tme):
    n_rows, d = xs.shape
    f = wg.shape[2]
    n_tiles = n_rows // tme
    tile = lambda i, te, nu: (jnp.minimum(i, nu[0] - 1), 0)
    grid_spec = pltpu.PrefetchScalarGridSpec(
        num_scalar_prefetch=2,
        grid=(n_tiles,),
        in_specs=[pl.BlockSpec((tme, d), tile),
                  pl.BlockSpec((None, d, f), lambda i, te, nu: (te[i], 0, 0)),
                  pl.BlockSpec((None, d, f), lambda i, te, nu: (te[i], 0, 0)),
                  pl.BlockSpec((None, f, d), lambda i, te, nu: (te[i], 0, 0))],
        out_specs=pl.BlockSpec((tme, d), lambda i, te, nu: (i, 0)),
        scratch_shapes=[pltpu.VMEM((d, f), BF16), pltpu.VMEM((d, f), BF16), pltpu.VMEM((f, d), BF16)],
    )
    return pl.pallas_call(
        _experts_kernel,
        grid_spec=grid_spec,
        out_shape=jax.ShapeDtypeStruct((n_rows, d), F32),
        compiler_params=_cparams(("arbitrary",)),
        name="moe_experts",
    )(tile_expert, n_used, xs, wg, wu, wd)


def _combine_kernel(pos_ref, posn_ref, x1_ref, wts_ref, gfin_ref, y_ref, o_ref, buf_ref, sem, *, tc, n_tiles):
    i = pl.program_id(0)

    def gather(slots_ref, s):
        for r in range(tc):
            for k in range(2):
                p = slots_ref[0, 0, k * tc + r]
                pltpu.make_async_copy(y_ref.at[pl.ds(p, 1)], buf_ref.at[s, k, pl.ds(r, 1)],
                                      sem.at[s]).start(priority=k)

    pl.when(i == 0)(functools.partial(gather, pos_ref, 0))
    for s in range(2):
        pl.when((i + 1 < n_tiles) & ((i + 1) % 2 == s))(functools.partial(gather, posn_ref, s))

    for s in range(2):
        @pl.when(i % 2 == s)
        def _(s=s):
            for k in range(2):
                pltpu.make_async_copy(y_ref.at[pl.ds(0, tc)], buf_ref.at[s, k], sem.at[s]).wait()
            wts = wts_ref[...]
            x2 = x1_ref[...] + wts[:, 0:1] * buf_ref[s, 0] + wts[:, 1:2] * buf_ref[s, 1]
            inv = lax.rsqrt(jnp.mean(x2 * x2, axis=-1, keepdims=True) + RMS_EPS)
            o_ref[...] = x2 * inv * gfin_ref[...]


def _combine(x1, wts, pos, y, gfin, tc):
    t, d = x1.shape
    n_tiles = t // tc
    pos3 = _tile_slots(pos, tc)
    slots = lambda nxt: pl.BlockSpec((1, 1, 2 * tc), lambda i: (jnp.minimum(i + nxt, n_tiles - 1), 0, 0),
                                     memory_space=pltpu.SMEM)
    return pl.pallas_call(
        functools.partial(_combine_kernel, tc=tc, n_tiles=n_tiles),
        grid=(n_tiles,),
        in_specs=[slots(0), slots(1),
                  pl.BlockSpec((tc, d), lambda i: (i, 0)),
                  pl.BlockSpec((tc, LANES), lambda i: (i, 0)),
                  _full((1, d)),
                  pl.BlockSpec(memory_space=pl.ANY)],
        out_specs=pl.BlockSpec((tc, d), lambda i: (i, 0)),
        out_shape=jax.ShapeDtypeStruct((t, d), F32),
        scratch_shapes=[pltpu.VMEM((2, 2, tc, d), F32), pltpu.SemaphoreType.DMA((2,))],
        compiler_params=_cparams(("arbitrary",)),
        name="moe_combine",
    )(pos3, pos3, x1, wts, gfin, y)


def _moe(groups, wg, wu, wd, gfin, tme, tr, td):
    sizes = [g[0].shape[0] for g in groups]
    t = sum(sizes)
    eid = jnp.concatenate([g[3] for g in groups], axis=0)
    rank, cnt = _rank(eid, min(tr, t))
    counts = cnt[0, :N_EXPERTS].astype(I32)
    padded = ((counts + tme - 1) // tme) * tme
    ends = jnp.cumsum(padded)
    offsets = ends - padded
    pos = (offsets[eid[:, :2]] + rank[:, :2]).astype(I32).T
    n_tiles = (2 * t + N_EXPERTS * (tme - 1)) // tme
    tile_start = jnp.arange(n_tiles, dtype=I32) * tme
    tile_expert = jnp.minimum(
        jnp.sum((ends[None, :] <= tile_start[:, None]).astype(I32), axis=1), N_EXPERTS - 1).astype(I32)
    n_used = (ends[-1:] // tme).astype(I32)
    starts =[sum(sizes[:i]) for i in range(len(sizes))]
    xs = jnp.zeros((n_tiles * tme, groups[0][1].shape[1]), F32)
    for (x1, h2, wts, _), s0, n in zip(groups, starts, sizes):
        xs = _dispatch(h2, pos[:, s0:s0 + n], xs, min(td, n))
    y = _experts(xs, tile_expert, n_used, wg, wu, wd, tme)
    return [_combine(x1, wts, pos[:, s0:s0 + n], y, gfin, min(td, n))
            for (x1, h2, wts, _), s0, n in zip(groups, starts, sizes)]


def kernel(x_prompt, x_sample, cache_k, cache_v, state_conv, page_table, norm_mix_g, w_in, conv_dw_w, conv_dw_b, conv_ln_g, conv_ln_b, conv_w_out, lambda_q1, lambda_k1, lambda_q2, lambda_k2, head_norm_g, attn_w_out, w_out, norm_ffn_g, router_group_w, router_group_b, router_expert_w, router_expert_b, expert_w_gate, expert_w_up, expert_w_down, final_norm_g):
    bsz, seq, d = x_prompt.shape
    db, ds, _ = x_sample.shape
    l = LAYER
    qkw = 2 * N_HEADS * HEAD_DIM
    vw = N_HEADS * V_DIM

    win = w_in[l]
    bounds = [0, D_CONV, 2 * D_CONV, 2 * D_CONV + qkw, 2 * D_CONV + 2 * qkw, 2 * D_CONV + 2 * qkw + vw,
              2 * D_CONV + 2 * qkw + vw + d, 2 * D_CONV + 2 * qkw + vw + 2 * d]
    w_all = jnp.concatenate([win[:, bounds[1]:bounds[2]], win[:, bounds[0]:bounds[1]], win[:, bounds[2]:]],
                            axis=1).astype(BF16)
    w_pw = conv_w_out[l].astype(BF16)
    w_ao = attn_w_out[l].astype(BF16)
    w_o = w_out[l].astype(BF16)
    w_r = jnp.zeros((d, LANES), F32).at[:, :N_GROUPS].set(router_group_w[l])
    w_r = w_r.at[:, N_GROUPS:N_GROUPS + N_EXPERTS].set(router_expert_w[l]).astype(BF16)
    b_r = jnp.zeros((1, LANES), F32).at[0, :N_GROUPS].set(router_group_b[l])
    b_r = b_r.at[0, N_GROUPS:N_GROUPS + N_EXPERTS].set(router_expert_b[l])
    wg, wu, wd = expert_w_gate[l], expert_w_up[l], expert_w_down[l]
    row = lambda a: a.reshape(1, -1)
    g_mix, g_ffn, g_fin = row(norm_mix_g[l]), row(norm_ffn_g[l]), row(final_norm_g)
    dww, dwb = conv_dw_w[l], row(conv_dw_b[l])
    lng, lnb = row(conv_ln_g[l]), row(conv_ln_b[l])
    lams = [row(lambda_q1[l]), row(lambda_k1[l]), row(lambda_q2[l]), row(lambda_k2[l])]
    g_head = row(head_norm_g[l])

    def project(x2d, tm):
        return _project(x2d, g_mix, w_all, tm)

    def merge(x2d, o, conv, sgc, sga):
        return _merge(x2d, o, conv, sgc, sga, w_ao, w_o, g_ffn, w_r, b_r)

    tp = bsz * seq
    xp = x_prompt.reshape(tp, d)
    u_p, q_p, k_p, kb_p, v_p, vb_p, sgc_p, sga_p = project(xp, 1024)
    conv_p = _conv_prompt(u_p.reshape(bsz, seq, D_CONV), dww, dwb, lng, lnb, w_pw).reshape(tp, d)
    qkv_p = (q_p.reshape(bsz, seq, qkw), kb_p.reshape(bsz, seq, qkw), vb_p.reshape(bsz, seq, vw))

    tsamp = db * ds
    xs_ = x_sample.reshape(tsamp, d)
    u_s, q_s, k_s, kb_s, v_s, vb_s, sgc_s, sga_s = project(xs_, tsamp)
    u_ext = jnp.concatenate([state_conv[l], u_s.reshape(db, ds, D_CONV)], axis=1)
    conv_s = _conv_sample(jnp.swapaxes(u_ext, 0, 1), dww, dwb, lng, lnb, w_pw)
    conv_s = jnp.swapaxes(conv_s, 0, 1).reshape(tsamp, d)
    q5 = q_s.reshape(db, ds, N_HEADS, 2, HEAD_DIM).transpose(0, 2, 3, 1, 4)
    zero = jnp.zeros_like(q5[:, :, 0])
    qbd = jnp.stack([jnp.concatenate([q5[:, :, 0], zero], axis=-1),
                     jnp.concatenate([zero, q5[:, :, 1]], axis=-1)], axis=2).reshape(db, N_HEADS * 2 * ds, V_DIM)
    pad_rows = NEW_PAD - ds * N_HEADS
    kn = jnp.pad(kb_s.reshape(db, ds * N_HEADS, V_DIM), ((0, 0), (0, pad_rows), (0, 0)))
    vn = jnp.pad(vb_s.reshape(db, ds * N_HEADS, V_DIM), ((0, 0), (0, pad_rows), (0, 0)))
    n_pool = cache_k.shape[1]
    ck = cache_k.reshape(cache_k.shape[0] * n_pool, PAGE_SIZE * N_HEADS, V_DIM)
    cv = cache_v.reshape(cache_v.shape[0] * n_pool, PAGE_SIZE * N_HEADS, V_DIM)
    tq = min(PROMPT_TQ, seq)
    pt = page_table + l * n_pool
    if db * (page_table.shape[1] // ATTN_PAGES) >= bsz * N_HEADS * (seq // tq):
        o_s, o_p = _attn_sample(qbd, kn, vn, ck, cv, pt, lams, g_head, prompt=(*qkv_p, tq), pp=ATTN_PAGES)
    else:
        (o_s,) = _attn_sample(qbd, kn, vn, ck, cv, pt, lams, g_head, pp=ATTN_PAGES)
        o_p = _attn_prompt(*qkv_p, lams, g_head, tq=tq)
    group_p = merge(xp, o_p.reshape(tp, vw), conv_p, sgc_p, sga_p)
    group_s = merge(xs_, o_s.reshape(tsamp, vw), conv_s, sgc_s, sga_s)

    t_all = tp + tsamp
    tr = next(c for c in (512, 256, 128) if t_all % c == 0)
    y_p, y_s = _moe([group_p, group_s], wg, wu, wd, g_fin, MOE_TILE, tr, 256)
    y_p = y_p.reshape(bsz, seq, d)
    y_s = y_s.reshape(db, ds, d)

    k_prompt = k_p.reshape(1, bsz, seq, N_HEADS, 2 * HEAD_DIM)
    v_prompt = v_p.reshape(1, bsz, seq, N_HEADS, V_DIM)
    conv_prompt = u_p.reshape(bsz, seq, D_CONV)[:, seq - (CONV_W - 1):][None]
    k_sample = k_s.reshape(1, db, ds, N_HEADS, 2 * HEAD_DIM)
    v_sample = v_s.reshape(1, db, ds, N_HEADS, V_DIM)
    conv_sample = u_ext[:, ds:][None]
    return (y_p, y_s, k_prompt, v_prompt, conv_prompt, k_sample, v_sample, conv_sample)
```

```python
import functools
import math

import jax
import jax.numpy as jnp
from jax import lax
from jax.experimental import pallas as pl
from jax.experimental.pallas import tpu as pltpu

F32 = jnp.float32
BF16 = jnp.bfloat16
I32 = jnp.int32

D_MODEL = 2048
HEAD_DIM = 128
N_HEADS = D_MODEL // (2 * HEAD_DIM)
V_DIM = 2 * HEAD_DIM
D_CONV = D_MODEL // 2
CONV_W = 31
N_GROUPS = 4
EXPERTS_PER_GROUP = 8
N_EXPERTS = N_GROUPS * EXPERTS_PER_GROUP
D_EXPERT = D_MODEL // 4
PAGE_SIZE = 128
RMS_EPS = 1e-6
LN_EPS = 1e-5
ALIBI_MAX_EXP = 8.0
LAYER = 0
LAM_INIT = 0.8 - 0.6 * math.exp(-0.3 * LAYER)
LOG2E = math.log2(math.e)

LANES = 128
SUBLANES = 8
NEG_BIG = -1e30
PROMPT_TQ = 512
ATTN_PAGES = 8
MOE_TILE = 512
VMEM_LIMIT = 56 * 1024 * 1024


def _cparams(sem):
    return pltpu.CompilerParams(dimension_semantics=sem, vmem_limit_bytes=VMEM_LIMIT)


def _full(shape):
    return pl.BlockSpec(shape, lambda *_: (0,) * len(shape))


PROJ_SEGMENTS = (("glu_b", D_CONV), ("glu_a", D_CONV), ("q", D_MODEL), ("k", D_MODEL), ("v", D_MODEL),
                 ("g_conv", D_MODEL), ("g_attn", D_MODEL))


def _proj_kernel(x_ref, g_ref, w_ref, u_ref, q_ref, k_ref, kb_ref, v_ref, vb_ref, gc_ref, ga_ref, h_ref, sig_ref,
                 *, tn):
    j = pl.program_id(1)

    @pl.when(j == 0)
    def _():
        x = x_ref[...]
        inv = lax.rsqrt(jnp.mean(x * x, axis=-1, keepdims=True) + RMS_EPS)
        h_ref[...] = (x * inv * g_ref[...]).astype(BF16)

    z = jnp.dot(h_ref[...], w_ref[...], preferred_element_type=F32)

    def glu_b(jj):
        sig_ref[jj] = jax.nn.sigmoid(z)

    def glu_a(jj):
        u_ref[...] = z * sig_ref[jj]

    def q_seg(jj):
        q_ref[...] = (z * (LOG2E * HEAD_DIM ** -0.5)).astype(q_ref.dtype)

    def k_seg(jj):
        k_ref[...] = z
        kb_ref[...] = z.astype(kb_ref.dtype)

    def v_seg(jj):
        v_ref[...] = z
        vb_ref[...] = z.astype(vb_ref.dtype)

    def gc_seg(jj):
        gc_ref[...] = jax.nn.sigmoid(z).astype(gc_ref.dtype)

    def ga_seg(jj):
        ga_ref[...] = jax.nn.sigmoid(z).astype(ga_ref.dtype)

    start = 0
    for (_, width), fn in zip(PROJ_SEGMENTS, (glu_b, glu_a, q_seg, k_seg, v_seg, gc_seg, ga_seg)):
        blocks = width // tn
        if fn in (glu_b, glu_a):
            for jj in range(blocks):
                pl.when(j == start + jj)(functools.partial(fn, jj))
        else:
            pl.when((j >= start) & (j < start + blocks))(functools.partial(fn, 0))
        start += blocks


def _project(x, g, w_all, tm, tn=512):
    t, d = x.shape
    n = w_all.shape[1]
    starts, s0 = {}, 0
    for name, width in PROJ_SEGMENTS:
        starts[name] = (s0 // tn, width // tn)
        s0 += width

    def seg_spec(name):
        first, blocks = starts[name]
        return pl.BlockSpec((tm, tn), lambda i, j: (i, jnp.clip(j - first, 0, blocks - 1)))

    outs = [("glu_a", D_CONV, F32), ("q", D_MODEL, BF16), ("k", D_MODEL, F32), ("k", D_MODEL, BF16),
            ("v", D_MODEL, F32), ("v", D_MODEL, BF16), ("g_conv", D_MODEL, BF16), ("g_attn", D_MODEL, BF16)]
    return pl.pallas_call(
        functools.partial(_proj_kernel, tn=tn),
        grid=(t // tm, n // tn),
        in_specs=[pl.BlockSpec((tm, d), lambda i, j: (i, 0)), _full((1, d)),
                  pl.BlockSpec((d, tn), lambda i, j: (0, j))],
        out_specs=[seg_spec(name) for name, _, _ in outs],
        out_shape=[jax.ShapeDtypeStruct((t, width), dt) for _, width, dt in outs],
        scratch_shapes=[pltpu.VMEM((tm, d), BF16), pltpu.VMEM((D_CONV // tn, tm, tn), F32)],
        compiler_params=_cparams(("arbitrary", "arbitrary")),
        name="proj",
    )(x, g, w_all)


def _ln_swish_project(z, lng_ref, lnb_ref, wpw_ref):
    mu = jnp.mean(z, axis=-1, keepdims=True)
    zc = z - mu
    var = jnp.mean(zc * zc, axis=-1, keepdims=True)
    y = zc * lax.rsqrt(var + LN_EPS) * lng_ref[...] + lnb_ref[...]
    y = y * jax.nn.sigmoid(y)
    return jnp.dot(y.astype(BF16), wpw_ref[...], preferred_element_type=F32)


HALO = 32
CONV_ROWS = 128


def _conv_prompt_kernel(u_ref, halo_ref, dww_ref, dwb_ref, lng_ref, lnb_ref, wpw_ref, o_ref, ext_ref, z_ref, sh_ref,
                        *, ts):
    first = pl.program_id(1) == 0
    ext_ref[0:HALO, :] = jnp.where(first, 0.0, halo_ref[...])
    ext_ref[HALO:, :] = u_ref[...]
    base = HALO - (CONV_W - 1)
    for cb in range(D_CONV // LANES):
        cs = slice(cb * LANES, (cb + 1) * LANES)
        for shift in range(SUBLANES):
            span = ts + HALO - (SUBLANES if shift else 0)
            sh_ref[shift, 0:span, :] = ext_ref[shift:shift + span, cs]
        for r0 in range(0, ts, CONV_ROWS):
            acc = jnp.broadcast_to(dwb_ref[:, cs], (CONV_ROWS, LANES))
            for j in range(CONV_W):
                shift = (base + j) % SUBLANES
                a = base + j - shift + r0
                acc = acc + dww_ref[j:j + 1, cs] * sh_ref[shift, a:a + CONV_ROWS, :]
            z_ref[r0:r0 + CONV_ROWS, cs] = acc
    o_ref[...] = _ln_swish_project(z_ref[...], lng_ref, lnb_ref, wpw_ref)


def _conv_prompt(u, dww, dwb, lng, lnb, wpw, ts=256):
    b, s, c = u.shape
    d = wpw.shape[1]
    r = ts // HALO
    return pl.pallas_call(
        functools.partial(_conv_prompt_kernel, ts=ts),
        grid=(b, s // ts),
        in_specs=[
            pl.BlockSpec((None, ts, c), lambda bi, i: (bi, i, 0)),
            pl.BlockSpec((None, HALO, c), lambda bi, i: (bi, jnp.maximum(i * r - 1, 0), 0)),
            _full((CONV_W, c)), _full((1, c)), _full((1, c)), _full((1, c)), _full((c, d)),
        ],
        out_specs=pl.BlockSpec((None, ts, d), lambda bi, i: (bi, i, 0)),
        out_shape=jax.ShapeDtypeStruct((b, s, d), F32),
        scratch_shapes=[pltpu.VMEM((ts + HALO, c), F32), pltpu.VMEM((ts, c), F32),
                        pltpu.VMEM((SUBLANES, ts + HALO, LANES), F32)],
        compiler_params=_cparams(("parallel", "arbitrary")),
        name="conv_prompt",
    )(u, u, dww, dwb, lng, lnb, wpw)


def _conv_sample_kernel(ext_ref, dww_ref, dwb_ref, lng_ref, lnb_ref, wpw_ref, o_ref, *, n_new, bt):
    zs = []
    for t in range(n_new):
        acc = jnp.broadcast_to(dwb_ref[...], (bt, D_CONV))
        for j in range(CONV_W):
            acc = acc + dww_ref[j:j + 1, :] * ext_ref[t + j]
        zs.append(acc)
    out = _ln_swish_project(jnp.concatenate(zs, axis=0), lng_ref, lnb_ref, wpw_ref)
    for t in range(n_new):
        o_ref[t] = out[t * bt:(t + 1) * bt]


def _conv_sample(ext_t, dww, dwb, lng, lnb, wpw, bt=32):
    rows, b, c = ext_t.shape
    n_new = rows - (CONV_W - 1)
    d = wpw.shape[1]
    return pl.pallas_call(
        functools.partial(_conv_sample_kernel, n_new=n_new, bt=bt),
        grid=(b // bt,),
        in_specs=[
            pl.BlockSpec((rows, bt, c), lambda i: (0, i, 0)),
            _full((CONV_W, c)), _full((1, c)), _full((1, c)), _full((1, c)), _full((c, d)),
        ],
        out_specs=pl.BlockSpec((n_new, bt, d), lambda i: (0, i, 0)),
        out_shape=jax.ShapeDtypeStruct((n_new, b, d), F32),
        compiler_params=_cparams(("parallel",)),
        name="conv_sample",
    )(ext_t, dww, dwb, lng, lnb, wpw)


def _lambda_value(lq1_ref, lk1_ref, lq2_ref, lk2_ref):
    a = jnp.sum(lq1_ref[...] * lk1_ref[...], axis=-1, keepdims=True)
    b = jnp.sum(lq2_ref[...] * lk2_ref[...], axis=-1, keepdims=True)
    return jnp.exp(a) - jnp.exp(b) + LAM_INIT


def _head_norm(o, g_ref):
    inv = lax.rsqrt(jnp.mean(o * o, axis=-1, keepdims=True) + RMS_EPS)
    return o * inv * g_ref[...] * (1.0 - LAM_INIT)


def _alibi_slope(h, shape):
    return jnp.exp2(jnp.full(shape, -(ALIBI_MAX_EXP / N_HEADS), F32) * (h + 1).astype(F32))


def _nt_dot(a, b):
    return lax.dot_general(a, b, (((1,), (1,)), ((), ())), preferred_element_type=F32)


def _prompt_query_tile(i, h, q_ref, k_ref, v_ref, lam_refs, g_ref, o_ref, *, tq):
    lo, hi = i * tq, (i + 1) * tq
    kpos = lax.broadcasted_iota(I32, (1, hi), 1).astype(F32)
    bias = _alibi_slope(h, (1, hi)) * LOG2E * kpos
    lam = _lambda_value(*lam_refs)
    visible = lax.broadcasted_iota(I32, (tq, tq), 1) <= lax.broadcasted_iota(I32, (tq, tq), 0)
    outs = []
    for c in range(2):
        cs = slice(c * HEAD_DIM, (c + 1) * HEAD_DIM)
        q = q_ref[lo:hi, cs]
        s_d = jnp.where(visible, _nt_dot(q, k_ref[lo:hi, cs]) + bias[:, lo:hi], -jnp.inf)
        m = jnp.max(s_d, axis=-1, keepdims=True)
        if i > 0:
            s_o = _nt_dot(q, k_ref[0:lo, cs]) + bias[:, 0:lo]
            m = jnp.maximum(m, jnp.max(s_o, axis=-1, keepdims=True))
        p_d = jnp.exp2(s_d - m)
        l = jnp.sum(p_d, axis=-1, keepdims=True)
        pv = jnp.dot(p_d.astype(BF16), v_ref[lo:hi, :], preferred_element_type=F32)
        if i > 0:
            p_o = jnp.exp2(s_o - m)
            l = l + jnp.sum(p_o, axis=-1, keepdims=True)
            pv = pv + jnp.dot(p_o.astype(BF16), v_ref[0:lo, :], preferred_element_type=F32)
        outs.append(pv / l)
    o_ref[lo:hi, :] = _head_norm(outs[0] - lam * outs[1], g_ref).astype(o_ref.dtype)


def _attn_prompt_kernel(q_ref, k_ref, v_ref, lq1_ref, lk1_ref, lq2_ref, lk2_ref, g_ref, o_ref, *, tq, seq):
    h = pl.program_id(1)
    for i in range(seq // tq):
        _prompt_query_tile(i, h, q_ref, k_ref, v_ref, (lq1_ref, lk1_ref, lq2_ref, lk2_ref), g_ref, o_ref, tq=tq)


def _attn_prompt(q, k, v, lams, g, tq=512):
    b, s, _ = q.shape
    w = 2 * HEAD_DIM
    spec = pl.BlockSpec((None, s, w), lambda bi, h: (bi, 0, h))
    return pl.pallas_call(
        functools.partial(_attn_prompt_kernel, tq=min(tq, s), seq=s),
        grid=(b, N_HEADS),
        in_specs=[spec, spec, spec] + [_full((1, HEAD_DIM))] * 4 + [_full((1, V_DIM))],
        out_specs=pl.BlockSpec((None, s, V_DIM), lambda bi, h: (bi, 0, h)),
        out_shape=jax.ShapeDtypeStruct((b, s, N_HEADS * V_DIM), BF16),
        compiler_params=_cparams(("parallel", "parallel")),
        name="attn_prompt",
    )(q, k, v, *lams, g)


NEW_PAD = 128


def _attn_sample_kernel(pt_ref, q_ref, kn_ref, vn_ref, bias_ref, off_ref, biasn_ref, *refs, pp, n_new, n_steps,
                        prompt_tiles):
    k_refs = refs[:pp]
    v_refs = refs[pp:2 * pp]
    lq1_ref, lk1_ref, lq2_ref, lk2_ref, g_ref = refs[2 * pp:2 * pp + 5]
    rest = refs[2 * pp + 5:]
    if prompt_tiles is not None:
        qp_ref, kp_ref, vp_ref, o_ref, op_ref, m_ref, l_ref, acc_ref = rest
    else:
        o_ref, m_ref, l_ref, acc_ref = rest
    step = pl.program_id(1)
    q = q_ref[...]

    @pl.when(step == 0)
    def _():
        m_ref[...] = jnp.full(m_ref.shape, NEG_BIG, F32)
        l_ref[...] = jnp.zeros(l_ref.shape, F32)
        acc_ref[...] = jnp.zeros(acc_ref.shape, F32)

    def update(c, s, off, v):
        m_prev = m_ref[c][:, :1]
        m_new = jnp.maximum(m_prev, jnp.max(s, axis=-1, keepdims=True) + off)
        alpha = jnp.exp2(m_prev - m_new)
        p = jnp.exp2(s - (m_new - off))
        l_new = alpha * l_ref[c][:, :1] + jnp.sum(p, axis=-1, keepdims=True)
        acc_ref[c] = alpha * acc_ref[c] + jnp.dot(p.astype(BF16), v, preferred_element_type=F32)
        m_ref[c] = jnp.broadcast_to(m_new, m_ref.shape[1:])
        l_ref[c] = jnp.broadcast_to(l_new, l_ref.shape[1:])

    def stream_pages():
        scores = [_nt_dot(q, k_refs[j][...].astype(BF16)) + bias_ref[...] for j in range(pp)]
        offs = [off_ref[:, :1] * (step * pp + j).astype(F32) for j in range(pp)]
        m_prevs = [m_ref[j][:, :1] for j in range(pp)]
        m_news = [jnp.maximum(m_prevs[j], jnp.max(scores[j], axis=-1, keepdims=True) + offs[j]) for j in range(pp)]
        ps = [jnp.exp2(scores[j] - (m_news[j] - offs[j])) for j in range(pp)]
        pvs = [jnp.dot(ps[j].astype(BF16), v_refs[j][...].astype(BF16), preferred_element_type=F32)
               for j in range(pp)]
        for j in range(pp):
            alpha = jnp.exp2(m_prevs[j] - m_news[j])
            l_new = alpha * l_ref[j][:, :1] + jnp.sum(ps[j], axis=-1, keepdims=True)
            acc_ref[j] = alpha * acc_ref[j] + pvs[j]
            m_ref[j] = jnp.broadcast_to(m_news[j], m_ref.shape[1:])
            l_ref[j] = jnp.broadcast_to(l_new, l_ref.shape[1:])

    if prompt_tiles is None:
        stream_pages()
    else:
        per_unit, tq, total = prompt_tiles
        flat = pl.program_id(0) * n_steps + step
        head = (flat // per_unit) % N_HEADS
        for i in range(per_unit):
            @pl.when((flat < total) & (flat % per_unit == i))
            def _(i=i):
                _prompt_query_tile(i, head, qp_ref, kp_ref, vp_ref, (lq1_ref, lk1_ref, lq2_ref, lk2_ref), g_ref,
                                   op_ref, tq=tq)
                stream_pages()

        pl.when(flat >= total)(stream_pages)

    @pl.when(step == n_steps - 1)
    def _():
        update(0, _nt_dot(q, kn_ref[...]) + biasn_ref[...], 0.0, vn_ref[...])
        m_all = m_ref[0][:, :1]
        for c in range(1, pp):
            m_all = jnp.maximum(m_all, m_ref[c][:, :1])
        l_all = jnp.zeros_like(m_all)
        acc_all = jnp.zeros(acc_ref.shape[1:], F32)
        for c in range(pp):
            w = jnp.exp2(m_ref[c][:, :1] - m_all)
            l_all = l_all + w * l_ref[c][:, :1]
            acc_all = acc_all + w * acc_ref[c]
        lam = _lambda_value(lq1_ref, lk1_ref, lq2_ref, lk2_ref)
        a = acc_all / l_all
        for h in range(N_HEADS):
            r0 = h * 2 * n_new
            o = a[r0:r0 + n_new] - lam * a[r0 + n_new:r0 + 2 * n_new]
            o_ref[:, h * V_DIM:(h + 1) * V_DIM] = _head_norm(o, g_ref).astype(o_ref.dtype)


def _attn_sample(q_all, kn, vn, cache_k, cache_v, page_table, lams, g, prompt=None, pp=8):
    db, n_pages = page_table.shape
    rows = q_all.shape[1]
    n_new = rows // (2 * N_HEADS)
    n_steps = n_pages // pp
    pr = PAGE_SIZE * N_HEADS

    slopes = 2.0 ** (-(ALIBI_MAX_EXP / N_HEADS) * (jnp.arange(rows, dtype=F32) // (2 * n_new) + 1.0))[:, None]
    row_head = (jnp.arange(rows, dtype=I32) // (2 * n_new))[:, None]
    row_q = (jnp.arange(rows, dtype=I32) % n_new)[:, None]
    col = jnp.arange(pr, dtype=I32)[None, :]
    bias = jnp.where(col % N_HEADS == row_head, slopes * LOG2E * (col // N_HEADS).astype(F32), -jnp.inf)
    off = jnp.broadcast_to(slopes * LOG2E * PAGE_SIZE, (rows, LANES))
    coln = jnp.arange(NEW_PAD, dtype=I32)[None, :]
    new_ok = (coln % N_HEADS == row_head) & (coln // N_HEADS <= row_q)
    biasn = jnp.where(new_ok, slopes * LOG2E * (n_pages * PAGE_SIZE + coln // N_HEADS).astype(F32), -jnp.inf)

    def page_spec(j):
        return pl.BlockSpec((None, pr, V_DIM), lambda b, s, pt: (pt[b * n_pages + s * pp + j], 0, 0))

    const = lambda shape: pl.BlockSpec(shape, lambda b, s, pt: (0,) * len(shape))
    per_seq = lambda r: pl.BlockSpec((None, r, V_DIM), lambda b, s, pt: (b, 0, 0))
    in_specs = ([per_seq(rows), per_seq(NEW_PAD), per_seq(NEW_PAD),
                 const((rows, pr)), const((rows, LANES)), const((rows, NEW_PAD))]
                + [page_spec(j) for j in range(pp)] + [page_spec(j) for j in range(pp)]
                + [const((1, HEAD_DIM))] * 4 + [const((1, V_DIM))])
    out_specs = [pl.BlockSpec((None, n_new, N_HEADS * V_DIM), lambda b, s, pt: (b, 0, 0))]
    out_shape = [jax.ShapeDtypeStruct((db, n_new, N_HEADS * V_DIM), BF16)]
    args = [q_all, kn, vn, bias, off, biasn, *([cache_k] * pp), *([cache_v] * pp), *lams, g]
    prompt_tiles = None
    if prompt is not None:
        qp, kp, vp, tq = prompt
        bsz, seq, _ = qp.shape
        per_unit = seq // tq
        units = bsz * N_HEADS
        prompt_tiles = (per_unit, tq, units * per_unit)
        assert db * n_steps >= units * per_unit, "not enough cache steps to carry the prompt tiles"

        def unit_map(b, s, pt):
            unit = jnp.minimum((b * n_steps + s) // per_unit, units - 1)
            return (unit // N_HEADS, 0, unit % N_HEADS)

        unit_spec = pl.BlockSpec((None, seq, V_DIM), unit_map)
        in_specs += [unit_spec] * 3
        out_specs.append(unit_spec)
        out_shape.append(jax.ShapeDtypeStruct((bsz, seq, N_HEADS * V_DIM), BF16))
        args += [qp, kp, vp]
    grid_spec = pltpu.PrefetchScalarGridSpec(
        num_scalar_prefetch=1,
        grid=(db, n_steps),
        in_specs=in_specs,
        out_specs=out_specs,
        scratch_shapes=[pltpu.VMEM((pp, rows, LANES), F32), pltpu.VMEM((pp, rows, LANES), F32),
                        pltpu.VMEM((pp, rows, V_DIM), F32)],
    )
    return pl.pallas_call(
        functools.partial(_attn_sample_kernel, pp=pp, n_new=n_new, n_steps=n_steps, prompt_tiles=prompt_tiles),
        grid_spec=grid_spec,
        out_shape=out_shape,
        compiler_params=_cparams(("arbitrary", "arbitrary")),
        name="attn_sample",
    )(page_table.reshape(-1), *args)


def _route(logits):
    lane = lax.broadcasted_iota(I32, logits.shape, 1)
    lane_f = lane.astype(F32)
    big = float(LANES)
    is_grp = lane < N_GROUPS
    lg = jnp.where(is_grp, logits, -jnp.inf)
    mg = jnp.max(lg, axis=-1, keepdims=True)
    g_sel = jnp.min(jnp.where(lg == mg, lane_f, big), axis=-1, keepdims=True)
    p_grp = 1.0 / jnp.sum(jnp.exp(lg - mg), axis=-1, keepdims=True)
    lo = N_GROUPS + EXPERTS_PER_GROUP * g_sel
    le = jnp.where((lane_f >= lo) & (lane_f < lo + EXPERTS_PER_GROUP), logits, -jnp.inf)
    m1 = jnp.max(le, axis=-1, keepdims=True)
    i1 = jnp.min(jnp.where(le == m1, lane_f, big), axis=-1, keepdims=True)
    le2 = jnp.where(lane_f == i1, -jnp.inf, le)
    m2 = jnp.max(le2, axis=-1, keepdims=True)
    i2 = jnp.min(jnp.where(le2 == m2, lane_f, big), axis=-1, keepdims=True)
    e2 = jnp.exp(m2 - m1)
    w1 = p_grp / (1.0 + e2)
    w2 = p_grp * e2 / (1.0 + e2)
    wts = jnp.where(lane == 0, w1, jnp.where(lane == 1, w2, 0.0))
    eid = jnp.where(lane == 0, i1 - N_GROUPS, jnp.where(lane == 1, i2 - N_GROUPS, 0.0)).astype(I32)
    return wts, eid


def _merge_kernel(x_ref, o_ref, conv_ref, sgc_ref, sga_ref, wa_ref, wo_ref, gf_ref, wr_ref, br_ref,
                  x1_ref, h2_ref, wts_ref, eid_ref):
    attn = jnp.dot(o_ref[...], wa_ref[...], preferred_element_type=F32)
    merged = sgc_ref[...].astype(F32) * conv_ref[...] + sga_ref[...].astype(F32) * attn
    x1 = x_ref[...] + jnp.dot(merged.astype(BF16), wo_ref[...], preferred_element_type=F32)
    x1_ref[...] = x1
    inv = lax.rsqrt(jnp.mean(x1 * x1, axis=-1, keepdims=True) + RMS_EPS)
    h2 = x1 * inv * gf_ref[...]
    h2_ref[...] = h2
    logits = jnp.dot(h2.astype(BF16), wr_ref[...], preferred_element_type=F32) + br_ref[...]
    wts, eid = _route(logits)
    wts_ref[...] = wts
    eid_ref[...] = eid


def _merge(x, o, conv, sgc, sga, wa, wo, gf, wr, br, tm=256):
    t, d = x.shape
    tm = min(tm, t)
    row = lambda w: pl.BlockSpec((tm, w), lambda i: (i, 0))
    single = lambda shape: pl.BlockSpec(shape, lambda i: (0,) * len(shape), pipeline_mode=pl.Buffered(1))
    return pl.pallas_call(
        _merge_kernel,
        grid=(t // tm,),
        in_specs=[row(d), row(d), row(d), row(d), row(d), single((d, d)), single((d, d)),
                  _full((1, d)), _full((d, LANES)), _full((1, LANES))],
        out_specs=[row(d), row(d), row(LANES), row(LANES)],
        out_shape=[jax.ShapeDtypeStruct((t, d), F32), jax.ShapeDtypeStruct((t, d), F32),
                   jax.ShapeDtypeStruct((t, LANES), F32), jax.ShapeDtypeStruct((t, LANES), I32)],
        compiler_params=_cparams(("parallel",)),
        name="merge_route",
    )(x, o, conv, sgc, sga, wa, wo, gf, wr, br)


def _rank_kernel(eid_ref, rank_ref, cnt_ref, carry_ref, *, tr):
    @pl.when(pl.program_id(0) == 0)
    def _():
        carry_ref[...] = jnp.zeros(carry_ref.shape, F32)

    eid = eid_ref[...]
    lane = lax.broadcasted_iota(I32, eid.shape, 1)
    e1 = jnp.sum(jnp.where(lane == 0, eid, 0), axis=-1, keepdims=True)
    e2 = jnp.sum(jnp.where(lane == 1, eid, 0), axis=-1, keepdims=True)
    hit1 = lane == e1
    hit2 = lane == e2
    onehot = jnp.where(hit1 | hit2, 1.0, 0.0)
    r = lax.broadcasted_iota(I32, (tr, tr), 0)
    c = lax.broadcasted_iota(I32, (tr, tr), 1)
    below = jnp.where(c < r, 1.0, 0.0).astype(BF16)
    carry = carry_ref[0:1, :]
    prefix = jnp.dot(below, onehot.astype(BF16), preferred_element_type=F32) + carry
    r1 = jnp.sum(jnp.where(hit1, prefix, 0.0), axis=-1, keepdims=True)
    r2 = jnp.sum(jnp.where(hit2, prefix, 0.0), axis=-1, keepdims=True)
    rank_ref[...] = jnp.where(lane == 0, r1, jnp.where(lane == 1, r2, 0.0)).astype(I32)
    total = carry + jnp.sum(onehot, axis=0, keepdims=True)
    carry_ref[...] = jnp.broadcast_to(total, carry_ref.shape)
    cnt_ref[...] = jnp.broadcast_to(total, cnt_ref.shape)


def _rank(eid, tr):
    t = eid.shape[0]
    return pl.pallas_call(
        functools.partial(_rank_kernel, tr=tr),
        grid=(t // tr,),
        in_specs=[pl.BlockSpec((tr, LANES), lambda i: (i, 0))],
        out_specs=[pl.BlockSpec((tr, LANES), lambda i: (i, 0)), _full((8, LANES))],
        out_shape=[jax.ShapeDtypeStruct((t, LANES), I32), jax.ShapeDtypeStruct((8, LANES), F32)],
        scratch_shapes=[pltpu.VMEM((8, LANES), F32)],
        compiler_params=_cparams(("arbitrary",)),
        name="expert_rank",
    )(eid)


def _tile_slots(pos, rows):
    t = pos.shape[1]
    return pos.reshape(2, t // rows, rows).transpose(1, 0, 2).reshape(t // rows, 1, 2 * rows)


def _dispatch_kernel(pos_ref, h_ref, xs_in_ref, xs_ref, sem, *, td):
    del xs_in_ref

    def row_copy(r, k):
        p = pos_ref[0, 0, k * td + r]
        return pltpu.make_async_copy(h_ref.at[pl.ds(r, 1)], xs_ref.at[pl.ds(p, 1)], sem)

    def start(r, carry):
        row_copy(r, 0).start()
        row_copy(r, 1).start(priority=1)
        return carry

    for r in range(td):
        start(r, 0)
    for _ in range(2):
        pltpu.make_async_copy(h_ref, xs_ref.at[pl.ds(0, td)], sem).wait()


def _dispatch(h2, pos, xs0, td):
    t, d = h2.shape
    pos3 = _tile_slots(pos, td)
    return pl.pallas_call(
        functools.partial(_dispatch_kernel, td=td),
        grid=(t // td,),
        in_specs=[pl.BlockSpec((1, 1, 2 * td), lambda i: (i, 0, 0), memory_space=pltpu.SMEM),
                  pl.BlockSpec((td, d), lambda i: (i, 0)),
                  pl.BlockSpec(memory_space=pl.ANY)],
        out_specs=pl.BlockSpec(memory_space=pl.ANY),
        out_shape=jax.ShapeDtypeStruct(xs0.shape, F32),
        scratch_shapes=[pltpu.SemaphoreType.DMA(())],
        input_output_aliases={2: 0},
        compiler_params=_cparams(("arbitrary",)),
        name="moe_dispatch",
    )(pos3, h2, xs0)


def _experts_kernel(te_ref, nu_ref, x_ref, wg_ref, wu_ref, wd_ref, y_ref, wgb_ref, wub_ref, wdb_ref):
    i = pl.program_id(0)
    active = i < nu_ref[0]
    new_expert = (i == 0) | (te_ref[i] != te_ref[jnp.maximum(i - 1, 0)])

    @pl.when(active & new_expert)
    def _():
        wgb_ref[...] = wg_ref[...].astype(BF16)
        wub_ref[...] = wu_ref[...].astype(BF16)
        wdb_ref[...] = wd_ref[...].astype(BF16)

    @pl.when(active)
    def _():
        x = x_ref[...].astype(BF16)
        hg = jnp.dot(x, wgb_ref[...], preferred_element_type=F32)
        hu = jnp.dot(x, wub_ref[...], preferred_element_type=F32)
        act = hg * jax.nn.sigmoid(hg) * hu
        y_ref[...] = jnp.dot(act.astype(BF16), wdb_ref[...], preferred_element_type=F32)

    @pl.when(pl.program_id(0) >= nu_ref[0])
    def _():
        y_ref[...] = jnp.zeros(y_ref.shape, F32)


def _experts(xs, tile_expert, n_used, wg, wu, wd, tme):
    n_rows, d = xs.shape
    f = wg.shape[2]
    n_tiles = n_rows // tme
    tile = lambda i, te, nu: (jnp.minimum(i, nu[0] - 1), 0)
    grid_spec = pltpu.PrefetchScalarGridSpec(
        num_scalar_prefetch=2,
        grid=(n_tiles,),
        in_specs=[pl.BlockSpec((tme, d), tile),
                  pl.BlockSpec((None, d, f), lambda i, te, nu: (te[i], 0, 0)),
                  pl.BlockSpec((None, d, f), lambda i, te, nu: (te[i], 0, 0)),
                  pl.BlockSpec((None, f, d), lambda i, te, nu: (te[i], 0, 0))],
        out_specs=pl.BlockSpec((tme, d), lambda i, te, nu: (i, 0)),
        scratch_shapes=[pltpu.VMEM((d, f), BF16), pltpu.VMEM((d, f), BF16), pltpu.VMEM((f, d), BF16)],
    )
    return pl.pallas_call(
        _experts_kernel,
        grid_spec=grid_spec,
        out_shape=jax.ShapeDtypeStruct((n_rows, d), F32),
        compiler_params=_cparams(("arbitrary",)),
        name="moe_experts",
    )(tile_expert, n_used, xs, wg, wu, wd)


def _combine_kernel(pos_ref, posn_ref, x1_ref, wts_ref, gfin_ref, y_ref, o_ref, buf_ref, sem, *, tc, n_tiles):
    i = pl.program_id(0)

    def gather(slots_ref, s):
        for r in range(tc):
            for k in range(2):
                p = slots_ref[0, 0, k * tc + r]
                pltpu.make_async_copy(y_ref.at[pl.ds(p, 1)], buf_ref.at[s, k, pl.ds(r, 1)],
                                      sem.at[s]).start(priority=k)

    pl.when(i == 0)(functools.partial(gather, pos_ref, 0))
    for s in range(2):
        pl.when((i + 1 < n_tiles) & ((i + 1) % 2 == s))(functools.partial(gather, posn_ref, s))

    for s in range(2):
        @pl.when(i % 2 == s)
        def _(s=s):
            for k in range(2):
                pltpu.make_async_copy(y_ref.at[pl.ds(0, tc)], buf_ref.at[s, k], sem.at[s]).wait()
            wts = wts_ref[...]
            x2 = x1_ref[...] + wts[:, 0:1] * buf_ref[s, 0] + wts[:, 1:2] * buf_ref[s, 1]
            inv = lax.rsqrt(jnp.mean(x2 * x2, axis=-1, keepdims=True) + RMS_EPS)
            o_ref[...] = x2 * inv * gfin_ref[...]


def _combine(x1, wts, pos, y, gfin, tc):
    t, d = x1.shape
    n_tiles = t // tc
    pos3 = _tile_slots(pos, tc)
    slots = lambda nxt: pl.BlockSpec((1, 1, 2 * tc), lambda i: (jnp.minimum(i + nxt, n_tiles - 1), 0, 0),
                                     memory_space=pltpu.SMEM)
    return pl.pallas_call(
        functools.partial(_combine_kernel, tc=tc, n_tiles=n_tiles),
        grid=(n_tiles,),
        in_specs=[slots(0), slots(1),
                  pl.BlockSpec((tc, d), lambda i: (i, 0)),
                  pl.BlockSpec((tc, LANES), lambda i: (i, 0)),
                  _full((1, d)),
                  pl.BlockSpec(memory_space=pl.ANY)],
        out_specs=pl.BlockSpec((tc, d), lambda i: (i, 0)),
        out_shape=jax.ShapeDtypeStruct((t, d), F32),
        scratch_shapes=[pltpu.VMEM((2, 2, tc, d), F32), pltpu.SemaphoreType.DMA((2,))],
        compiler_params=_cparams(("arbitrary",)),
        name="moe_combine",
    )(pos3, pos3, x1, wts, gfin, y)


def _moe(groups, wg, wu, wd, gfin, tme, tr, td):
    sizes = [g[0].shape[0] for g in groups]
    t = sum(sizes)
    eid = jnp.concatenate([g[3] for g in groups], axis=0)
    rank, cnt = _rank(eid, min(tr, t))
    counts = cnt[0, :N_EXPERTS].astype(I32)
    padded = ((counts + tme - 1) // tme) * tme
    ends = jnp.cumsum(padded)
    offsets = ends - padded
    pos = (offsets[eid[:, :2]] + rank[:, :2]).astype(I32).T
    n_tiles = (2 * t + N_EXPERTS * (tme - 1)) // tme
    tile_start = jnp.arange(n_tiles, dtype=I32) * tme
    tile_expert = jnp.minimum(
        jnp.sum((ends[None, :] <= tile_start[:, None]).astype(I32), axis=1), N_EXPERTS - 1).astype(I32)
    n_used = (ends[-1:] // tme).astype(I32)
    starts =[sum(sizes[:i]) for i in range(len(sizes))]
    xs = jnp.zeros((n_tiles * tme, groups[0][1].shape[1]), F32)
    for (x1, h2, wts, _), s0, n in zip(groups, starts, sizes):
        xs = _dispatch(h2, pos[:, s0:s0 + n], xs, min(td, n))
    y = _experts(xs, tile_expert, n_used, wg, wu, wd, tme)
    return [_combine(x1, wts, pos[:, s0:s0 + n], y, gfin, min(td, n))
            for (x1, h2, wts, _), s0, n in zip(groups, starts, sizes)]


def kernel(x_prompt, x_sample, cache_k, cache_v, state_conv, page_table, norm_mix_g, w_in, conv_dw_w, conv_dw_b, conv_ln_g, conv_ln_b, conv_w_out, lambda_q1, lambda_k1, lambda_q2, lambda_k2, head_norm_g, attn_w_out, w_out, norm_ffn_g, router_group_w, router_group_b, router_expert_w, router_expert_b, expert_w_gate, expert_w_up, expert_w_down, final_norm_g):
    bsz, seq, d = x_prompt.shape
    db, ds, _ = x_sample.shape
    l = LAYER
    qkw = 2 * N_HEADS * HEAD_DIM
    vw = N_HEADS * V_DIM

    win = w_in[l]
    bounds = [0, D_CONV, 2 * D_CONV, 2 * D_CONV + qkw, 2 * D_CONV + 2 * qkw, 2 * D_CONV + 2 * qkw + vw,
              2 * D_CONV + 2 * qkw + vw + d, 2 * D_CONV + 2 * qkw + vw + 2 * d]
    w_all = jnp.concatenate([win[:, bounds[1]:bounds[2]], win[:, bounds[0]:bounds[1]], win[:, bounds[2]:]],
                            axis=1).astype(BF16)
    w_pw = conv_w_out[l].astype(BF16)
    w_ao = attn_w_out[l].astype(BF16)
    w_o = w_out[l].astype(BF16)
    w_r = jnp.zeros((d, LANES), F32).at[:, :N_GROUPS].set(router_group_w[l])
    w_r = w_r.at[:, N_GROUPS:N_GROUPS + N_EXPERTS].set(router_expert_w[l]).astype(BF16)
    b_r = jnp.zeros((1, LANES), F32).at[0, :N_GROUPS].set(router_group_b[l])
    b_r = b_r.at[0, N_GROUPS:N_GROUPS + N_EXPERTS].set(router_expert_b[l])
    wg, wu, wd = expert_w_gate[l], expert_w_up[l], expert_w_down[l]
    row = lambda a: a.reshape(1, -1)
    g_mix, g_ffn, g_fin = row(norm_mix_g[l]), row(norm_ffn_g[l]), row(final_norm_g)
    dww, dwb = conv_dw_w[l], row(conv_dw_b[l])
    lng, lnb = row(conv_ln_g[l]), row(conv_ln_b[l])
    lams = [row(lambda_q1[l]), row(lambda_k1[l]), row(lambda_q2[l]), row(lambda_k2[l])]
    g_head = row(head_norm_g[l])

    def project(x2d, tm):
        return _project(x2d, g_mix, w_all, tm)

    def merge(x2d, o, conv, sgc, sga):
        return _merge(x2d, o, conv, sgc, sga, w_ao, w_o, g_ffn, w_r, b_r)

    tp = bsz * seq
    xp = x_prompt.reshape(tp, d)
    u_p, q_p, k_p, kb_p, v_p, vb_p, sgc_p, sga_p = project(xp, 1024)
    conv_p = _conv_prompt(u_p.reshape(bsz, seq, D_CONV), dww, dwb, lng, lnb, w_pw).reshape(tp, d)
    qkv_p = (q_p.reshape(bsz, seq, qkw), kb_p.reshape(bsz, seq, qkw), vb_p.reshape(bsz, seq, vw))

    tsamp = db * ds
    xs_ = x_sample.reshape(tsamp, d)
    u_s, q_s, k_s, kb_s, v_s, vb_s, sgc_s, sga_s = project(xs_, tsamp)
    u_ext = jnp.concatenate([state_conv[l], u_s.reshape(db, ds, D_CONV)], axis=1)
    conv_s = _conv_sample(jnp.swapaxes(u_ext, 0, 1), dww, dwb, lng, lnb, w_pw)
    conv_s = jnp.swapaxes(conv_s, 0, 1).reshape(tsamp, d)
    q5 = q_s.reshape(db, ds, N_HEADS, 2, HEAD_DIM).transpose(0, 2, 3, 1, 4)
    zero = jnp.zeros_like(q5[:, :, 0])
    qbd = jnp.stack([jnp.concatenate([q5[:, :, 0], zero], axis=-1),
                     jnp.concatenate([zero, q5[:, :, 1]], axis=-1)], axis=2).reshape(db, N_HEADS * 2 * ds, V_DIM)
    pad_rows = NEW_PAD - ds * N_HEADS
    kn = jnp.pad(kb_s.reshape(db, ds * N_HEADS, V_DIM), ((0, 0), (0, pad_rows), (0, 0)))
    vn = jnp.pad(vb_s.reshape(db, ds * N_HEADS, V_DIM), ((0, 0), (0, pad_rows), (0, 0)))
    n_pool = cache_k.shape[1]
    ck = cache_k.reshape(cache_k.shape[0] * n_pool, PAGE_SIZE * N_HEADS, V_DIM)
    cv = cache_v.reshape(cache_v.shape[0] * n_pool, PAGE_SIZE * N_HEADS, V_DIM)
    tq = min(PROMPT_TQ, seq)
    pt = page_table + l * n_pool
    if db * (page_table.shape[1] // ATTN_PAGES) >= bsz * N_HEADS * (seq // tq):
        o_s, o_p = _attn_sample(qbd, kn, vn, ck, cv, pt, lams, g_head, prompt=(*qkv_p, tq), pp=ATTN_PAGES)
    else:
        (o_s,) = _attn_sample(qbd, kn, vn, ck, cv, pt, lams, g_head, pp=ATTN_PAGES)
        o_p = _attn_prompt(*qkv_p, lams, g_head, tq=tq)
    group_p = merge(xp, o_p.reshape(tp, vw), conv_p, sgc_p, sga_p)
    group_s = merge(xs_, o_s.reshape(tsamp, vw), conv_s, sgc_s, sga_s)

    t_all = tp + tsamp
    tr = next(c for c in (512, 256, 128) if t_all % c == 0)
    y_p, y_s = _moe([group_p, group_s], wg, wu, wd, g_fin, MOE_TILE, tr, 512)
    y_p = y_p.reshape(bsz, seq, d)
    y_s = y_s.reshape(db, ds, d)

    k_prompt = k_p.reshape(1, bsz, seq, N_HEADS, 2 * HEAD_DIM)
    v_prompt = v_p.reshape(1, bsz, seq, N_HEADS, V_DIM)
    conv_prompt = u_p.reshape(bsz, seq, D_CONV)[:, seq - (CONV_W - 1):][None]
    k_sample = k_s.reshape(1, db, ds, N_HEADS, 2 * HEAD_DIM)
    v_sample = v_s.reshape(1, db, ds, N_HEADS, V_DIM)
    conv_sample = u_ext[:, ds:][None]
    return (y_p, y_s, k_prompt, v_prompt, conv_prompt, k_sample, v_sample, conv_sample)
```
